```python
import jax, jax.numpy as jnp
from jax import lax
import numpy as np

D_MODEL = 1024
BATCH = 8
SEQ = 2048
DEPTH = 2
DEC_BATCH = 128
DEC_SEQ = 8
PAST_LEN = 16384
PAGE_SIZE = 128

HEAD_DIM = 64
N_HEADS_A = D_MODEL // HEAD_DIM
N_KV_A = max(N_HEADS_A // 8, 1)
GP_A = N_HEADS_A // N_KV_A
WIN_A = 128
N_HEADS_B = D_MODEL // HEAD_DIM
N_KV_B = 2
GP_B = N_HEADS_B // N_KV_B
N_BRANCH = 3
CMP_LEN = 32
CMP_STRIDE = 16
CMP_R = CMP_LEN // CMP_STRIDE
CMP_HID = HEAD_DIM
SLC_BLK = 64
N_SEL = 16
WIN_B = 512
ROPE_THETA = 500000.0
ROT_DIM = HEAD_DIM // 4
D_FF = ((8 * D_MODEL + 3 * 256 - 1) // (3 * 256)) * 256
N_LAYERS_A = DEPTH // 2
N_LAYERS_B = DEPTH - N_LAYERS_A
Q_BLK = 128
SEL_QBLK = 64
EPS = 1e-6
FORCE_SCORE = 1e9
SCALE = HEAD_DIM ** -0.5

kernel_name = 'yoco_swa_sink_nsa_decode_step'


def rms_norm(x, g):
    xf = x.astype(jnp.float32)
    y = xf * lax.rsqrt(jnp.mean(xf * xf, axis=-1, keepdims=True) + EPS)
    return (y * g.astype(jnp.float32)).astype(x.dtype)


def rotary(x, pos):
    half = ROT_DIM // 2
    inv = jnp.power(ROPE_THETA, -jnp.arange(half, dtype=jnp.float32) * (2.0 / ROT_DIM))
    ang = pos.astype(jnp.float32)[:, None] * inv[None, :]
    cos = jnp.cos(ang)[:, None, :]
    sin = jnp.sin(ang)[:, None, :]
    xr = x[..., :ROT_DIM].astype(jnp.float32)
    x1, x2 = xr[..., :half], xr[..., half:]
    rot = jnp.concatenate([x1 * cos - x2 * sin, x2 * cos + x1 * sin], axis=-1)
    return jnp.concatenate([rot.astype(x.dtype), x[..., ROT_DIM:]], axis=-1)


def masked_softmax(s, mask, sink=None):
    s = jnp.where(mask, s, -jnp.inf)
    m = jnp.max(s, axis=-1, keepdims=True)
    if sink is not None:
        m = jnp.maximum(m, sink)
    m = jnp.where(jnp.isfinite(m), m, 0.0)
    e = jnp.exp(s - m)
    den = jnp.sum(e, axis=-1, keepdims=True)
    if sink is not None:
        den = den + jnp.exp(sink - m)
    return e / jnp.where(den > 0, den, 1.0)


def band_attention(q, k, v, window, sink=None):
    b, s_len, kvh, gp, hd = q.shape
    nb = s_len // Q_BLK
    nprev = -(-(window - 1) // Q_BLK)
    pad = nprev * Q_BLK
    widths = ((0, 0), (pad, 0), (0, 0), (0, 0))
    kp = jnp.pad(k, widths).reshape(b, nb + nprev, Q_BLK, kvh, hd)
    vp = jnp.pad(v, widths).reshape(b, nb + nprev, Q_BLK, kvh, hd)
    kb = jnp.concatenate([kp[:, i:i + nb] for i in range(nprev + 1)], axis=2)
    vb = jnp.concatenate([vp[:, i:i + nb] for i in range(nprev + 1)], axis=2)
    qb = q.reshape(b, nb, Q_BLK, kvh, gp, hd)
    s = jnp.einsum('bnqgpd,bnkgd->bngpqk', qb, kb).astype(jnp.float32) * SCALE
    qpos = jnp.arange(nb)[:, None] * Q_BLK + jnp.arange(Q_BLK)[None, :]
    kpos = jnp.arange(nb)[:, None] * Q_BLK - pad + jnp.arange((nprev + 1) * Q_BLK)[None, :]
    diff = qpos[:, :, None] - kpos[:, None, :]
    mask = (diff >= 0) & (diff < window) & (kpos[:, None, :] >= 0)
    mask = mask[None, :, None, None]
    sk = None if sink is None else sink.astype(jnp.float32).reshape(1, 1, kvh, gp, 1, 1)
    p = masked_softmax(s, mask, sk)
    o = jnp.einsum('bngpqk,bnkgd->bnqgpd', p.astype(v.dtype), vb)
    return o.reshape(b, s_len, kvh, gp, hd)


def window_decode(q, buf, rows_new, window, past_len, sink=None):
    t = q.shape[1]
    wb = buf.shape[1]
    kvh, gp = q.shape[2], q.shape[3]
    kv = jnp.concatenate([buf, rows_new], axis=1)
    k, v = kv[:, :, 0], kv[:, :, 1]
    s = jnp.einsum('btgpd,bkgd->bgptk', q, k).astype(jnp.float32) * SCALE
    kpos = past_len - wb + jnp.arange(wb + t)
    qpos = past_len + jnp.arange(t)
    diff = qpos[:, None] - kpos[None, :]
    mask = ((diff >= 0) & (diff < window))[None, None, None]
    sk = None if sink is None else sink.astype(jnp.float32).reshape(1, kvh, gp, 1, 1)
    p = masked_softmax(s, mask, sk)
    return jnp.einsum('bgptk,bkgd->btgpd', p.astype(v.dtype), v)


def compress_partials(rows, w1):
    b, nr = rows.shape[:2]
    ch = rows.reshape(b, nr // CMP_STRIDE, CMP_STRIDE, 2, N_KV_B, HEAD_DIM)
    w = w1.reshape(2, CMP_R, CMP_STRIDE, HEAD_DIM, CMP_HID)
    return jnp.einsum('bclskd,srldh->bcskrh', ch, w)


def compress_finish(part, w1, b1, pe, w2):
    nc = part.shape[1] - CMP_R + 1
    h = part[:, 0:nc, :, :, 0]
    for r in range(1, CMP_R):
        h = h + part[:, r:r + nc, :, :, r]
    bias = jnp.einsum('sld,sldh->sh', pe, w1.reshape(2, CMP_LEN, HEAD_DIM, CMP_HID)) + b1
    h = jax.nn.gelu(h + bias[:, None, :])
    out = jnp.einsum('bcskh,shd->bcskd', h, w2)
    return out[:, :, 0], out[:, :, 1]


def cmp_attention(q, kc, vc, qpos):
    s = jnp.einsum('btgpd,bcgd->bgptc', q, kc).astype(jnp.float32) * SCALE
    nc = kc.shape[1]
    mask = (jnp.arange(nc) * CMP_STRIDE + CMP_LEN - 1)[None, :] <= qpos[:, None]
    p = masked_softmax(s, mask)
    o = jnp.einsum('bgptc,bcgd->btgpd', p.astype(vc.dtype), vc)
    return o, p


def select_blocks(p, qpos, ns):
    nc = p.shape[-1]
    ci = jnp.arange(nc) * CMP_STRIDE
    sj = jnp.arange(ns) * SLC_BLK
    overlap = ((ci[:, None] < sj[None, :] + SLC_BLK) & (ci[:, None] + CMP_LEN > sj[None, :])).astype(jnp.float32)
    imp = jnp.einsum('bgptc,cj->bgtj', p, overlap)
    cur = qpos // SLC_BLK
    j = jnp.arange(ns)[None, :]
    forced = (j == 0) | (j == cur[:, None]) | (j == cur[:, None] - 1)
    valid = j <= cur[:, None]
    score = jnp.where(forced, FORCE_SCORE, jnp.where(valid, imp, -1.0))
    _, idx = lax.top_k(score, min(N_SEL, ns))
    sel_valid = idx <= cur[None, None, :, None]
    return idx, sel_valid


def sel_core(q, kg, vg, idx, valid, qpos):
    s = jnp.einsum('btgpd,bgtnld->bgptnl', q, kg).astype(jnp.float32) * SCALE
    b, kvh, gp, t, n, l = s.shape
    kpos = idx[..., None] * SLC_BLK + jnp.arange(SLC_BLK)
    mask = valid[..., None] & (kpos <= qpos[None, None, :, None, None])
    p = masked_softmax(s.reshape(b, kvh, gp, t, n * l), mask[:, :, None].reshape(b, kvh, 1, t, n * l))
    return jnp.einsum('bgptnl,bgtnld->btgpd', p.reshape(s.shape).astype(vg.dtype), vg)


def slc_attention_prompt(q, k, v, idx, valid):
    b, s_len, kvh, gp, hd = q.shape
    ns = s_len // SLC_BLK
    n = idx.shape[-1]
    nqb = s_len // SEL_QBLK
    kb = k.reshape(b, ns, SLC_BLK, kvh, hd)
    vb = v.reshape(b, ns, SLC_BLK, kvh, hd)
    qs = q.reshape(b, nqb, SEL_QBLK, kvh, gp, hd).transpose(1, 0, 2, 3, 4, 5)
    idxs = idx.reshape(b, kvh, nqb, SEL_QBLK, n).transpose(2, 0, 1, 3, 4)
    vals = valid.reshape(b, kvh, nqb, SEL_QBLK, n).transpose(2, 0, 1, 3, 4)
    qpos = jnp.arange(s_len).reshape(nqb, SEL_QBLK)
    bi = jnp.arange(b)[:, None, None, None]
    gi = jnp.arange(kvh)[None, :, None, None]

    def one(args):
        qc, ic, vc_, pc = args
        kg = kb[bi, ic, :, gi, :]
        vg = vb[bi, ic, :, gi, :]
        return sel_core(qc, kg, vg, ic, vc_, pc)

    o = lax.map(one, (qs, idxs, vals, qpos))
    return o.transpose(1, 0, 2, 3, 4, 5).reshape(b, s_len, kvh, gp, hd)


def slc_gather_decode(pool, rows_new, page_table, idx, past_len):
    db, t = rows_new.shape[:2]
    kvh = rows_new.shape[3]
    bpp = PAGE_SIZE // SLC_BLK
    past_blocks = past_len // SLC_BLK
    poolb = pool.reshape(pool.shape[0], bpp, SLC_BLK, 2, kvh, HEAD_DIM)
    bi = jnp.arange(db)[:, None, None, None]
    gi = jnp.arange(kvh)[None, :, None, None]
    ip = jnp.minimum(idx, past_blocks - 1)
    page = page_table[bi, ip // bpp]
    g_past = poolb[page, ip % bpp, :, :, gi, :]
    nnb = -(-t // SLC_BLK)
    newp = jnp.pad(rows_new, ((0, 0), (0, nnb * SLC_BLK - t), (0, 0), (0, 0), (0, 0)))
    newp = newp.reshape(db, nnb, SLC_BLK, 2, kvh, HEAD_DIM)
    inew = jnp.clip(idx - past_blocks, 0, nnb - 1)
    g_new = newp[bi, inew, :, :, gi, :]
    g = jnp.where((idx < past_blocks)[..., None, None, None], g_past, g_new)
    return g[..., 0, :], g[..., 1, :]


def a_project(h, norm1, wqkv, qn, kn, pos):
    b, t, _ = h.shape
    proj = rms_norm(h, norm1) @ wqkv
    nq = N_HEADS_A * HEAD_DIM
    nk = N_KV_A * HEAD_DIM
    q = rms_norm(proj[..., :nq].reshape(b, t, N_HEADS_A, HEAD_DIM), qn)
    k = rms_norm(proj[..., nq:nq + nk].reshape(b, t, N_KV_A, HEAD_DIM), kn)
    v = proj[..., nq + nk:].reshape(b, t, N_KV_A, HEAD_DIM)
    q = rotary(q, pos).reshape(b, t, N_KV_A, GP_A, HEAD_DIM)
    k = rotary(k, pos)
    return q, k, v


def b_query(h, norm1, wq, qn, pos):
    b, t, _ = h.shape
    proj = rms_norm(h, norm1) @ wq
    nq = N_HEADS_B * HEAD_DIM
    q = rms_norm(proj[..., :nq].reshape(b, t, N_HEADS_B, HEAD_DIM), qn)
    q = rotary(q, pos).reshape(b, t, N_KV_B, GP_B, HEAD_DIM)
    gates = jax.nn.sigmoid(proj[..., nq:].astype(jnp.float32)).reshape(b, t, N_BRANCH, N_KV_B, GP_B)
    return q, gates


def shared_kv(h, kv_norm, kv_w, kv_knorm, pos):
    b, t, _ = h.shape
    kv = (rms_norm(h, kv_norm) @ kv_w).reshape(b, t, N_BRANCH, 2, N_KV_B, HEAD_DIM)
    k = rms_norm(kv[:, :, :, 0], kv_knorm[:, None, :])
    k = rotary(k.reshape(b, t, N_BRANCH * N_KV_B, HEAD_DIM), pos).reshape(b, t, N_BRANCH, N_KV_B, HEAD_DIM)
    rows = jnp.stack([k, kv[:, :, :, 1]], axis=3)
    return rows[:, :, 0], rows[:, :, 1], rows[:, :, 2]


def nsa_merge(gates, o_c, o_s, o_w, wo, dtype):
    g = gates[..., None]
    o = g[:, :, 0] * o_c + g[:, :, 1] * o_s + g[:, :, 2] * o_w
    b, t = o.shape[:2]
    return o.reshape(b, t, -1).astype(dtype) @ wo


def swiglu(h, norm2, w13, w2):
    gu = rms_norm(h, norm2) @ w13
    g, u = jnp.split(gu, 2, axis=-1)
    return h + (jax.nn.silu(g) * u) @ w2


def setup_inputs(seed: int = 0) -> dict:
    key = jax.random.key(seed)
    ks = iter(jax.random.split(key, 32))
    n_pages = PAST_LEN // PAGE_SIZE
    n_pool = (DEC_BATCH * n_pages * 5 + 3) // 4

    def nrm(shape, scale):
        return scale * jax.random.normal(next(ks), shape, jnp.float32)

    def gain(shape):
        return 1.0 + 0.05 * jax.random.normal(next(ks), shape, jnp.float32)

    wa = min(WIN_A, PAST_LEN)
    wb = min(WIN_B, PAST_LEN)
    na, nb_ = N_LAYERS_A, N_LAYERS_B
    d = D_MODEL
    inp = {}
    inp['x_prompt'] = nrm((BATCH, SEQ, d), 1.0)
    inp['x_sample'] = nrm((DEC_BATCH, DEC_SEQ, d), 1.0)
    inp['cache_swa_a'] = nrm((na, DEC_BATCH, wa, 2, N_KV_A, HEAD_DIM), 1.0)
    inp['cache_cmp'] = nrm((n_pool, PAGE_SIZE, 2, N_KV_B, HEAD_DIM), 1.0)
    inp['cache_slc'] = nrm((n_pool, PAGE_SIZE, 2, N_KV_B, HEAD_DIM), 1.0)
    inp['cache_win_b'] = nrm((DEC_BATCH, wb, 2, N_KV_B, HEAD_DIM), 1.0)
    perm = jax.random.permutation(next(ks), n_pool)
    inp['page_table'] = perm[:DEC_BATCH * n_pages].reshape(DEC_BATCH, n_pages).astype(jnp.int32)
    qkv_a = (N_HEADS_A + 2 * N_KV_A) * HEAD_DIM
    inp['a_norm1'] = gain((na, d))
    inp['a_wqkv'] = nrm((na, d, qkv_a), d ** -0.5)
    inp['a_qnorm'] = gain((na, HEAD_DIM))
    inp['a_knorm'] = gain((na, HEAD_DIM))
    inp['a_sink'] = nrm((na, N_HEADS_A), 0.5)
    inp['a_wo'] = nrm((na, N_HEADS_A * HEAD_DIM, d), (N_HEADS_A * HEAD_DIM) ** -0.5)
    inp['a_norm2'] = gain((na, d))
    inp['a_w13'] = nrm((na, d, 2 * D_FF), d ** -0.5)
    inp['a_w2'] = nrm((na, D_FF, d), D_FF ** -0.5)
    inp['kv_norm'] = gain((d,))
    inp['kv_w'] = nrm((d, N_BRANCH * 2 * N_KV_B * HEAD_DIM), d ** -0.5)
    inp['kv_knorm'] = gain((N_BRANCH, HEAD_DIM))
    inp['cmp_pe'] = nrm((2, CMP_LEN, HEAD_DIM), 0.1)
    inp['cmp_w1'] = nrm((2, CMP_LEN * HEAD_DIM, CMP_HID), (CMP_LEN * HEAD_DIM) ** -0.5)
    inp['cmp_b1'] = nrm((2, CMP_HID), 0.01)
    inp['cmp_w2'] = nrm((2, CMP_HID, HEAD_DIM), CMP_HID ** -0.5)
    inp['b_norm1'] = gain((nb_, d))
    inp['b_wq'] = nrm((nb_, d, N_HEADS_B * HEAD_DIM + N_BRANCH * N_HEADS_B), d ** -0.5)
    inp['b_qnorm'] = gain((nb_, HEAD_DIM))
    inp['b_wo'] = nrm((nb_, N_HEADS_B * HEAD_DIM, d), (N_HEADS_B * HEAD_DIM) ** -0.5)
    inp['b_norm2'] = gain((nb_, d))
    inp['b_w13'] = nrm((nb_, d, 2 * D_FF), d ** -0.5)
    inp['b_w2'] = nrm((nb_, D_FF, d), D_FF ** -0.5)
    return inp


def reference(x_prompt, x_sample, cache_swa_a, cache_cmp, cache_slc, cache_win_b, page_table,
              a_norm1, a_wqkv, a_qnorm, a_knorm, a_sink, a_wo, a_norm2, a_w13, a_w2,
              kv_norm, kv_w, kv_knorm, cmp_pe, cmp_w1, cmp_b1, cmp_w2,
              b_norm1, b_wq, b_qnorm, b_wo, b_norm2, b_w13, b_w2):
    b_p, s_len = x_prompt.shape[0], x_prompt.shape[1]
    b_s, t_len = x_sample.shape[0], x_sample.shape[1]
    past_len = page_table.shape[1] * PAGE_SIZE
    pos_p = jnp.arange(s_len, dtype=jnp.int32)
    pos_s = past_len + jnp.arange(t_len, dtype=jnp.int32)
    hp, hs = x_prompt, x_sample
    swa_p, swa_s = [], []
    for l in range(DEPTH):
        if l < N_LAYERS_A:
            q, k, v = a_project(hp, a_norm1[l], a_wqkv[l], a_qnorm[l], a_knorm[l], pos_p)
            o = band_attention(q, k, v, WIN_A, a_sink[l])
            hp = hp + o.reshape(b_p, s_len, -1) @ a_wo[l]
            swa_p.append(jnp.stack([k, v], axis=2)[:, -min(WIN_A, s_len):])
            q, k, v = a_project(hs, a_norm1[l], a_wqkv[l], a_qnorm[l], a_knorm[l], pos_s)
            rows = jnp.stack([k, v], axis=2)
            o = window_decode(q, cache_swa_a[l], rows, WIN_A, past_len, a_sink[l])
            hs = hs + o.reshape(b_s, t_len, -1) @ a_wo[l]
            swa_s.append(jnp.concatenate([cache_swa_a[l], rows], axis=1)[:, -cache_swa_a.shape[2]:])
            hp = swiglu(hp, a_norm2[l], a_w13[l], a_w2[l])
            hs = swiglu(hs, a_norm2[l], a_w13[l], a_w2[l])
        else:
            if l == N_LAYERS_A:
                cmp_p, slc_p, win_p = shared_kv(hp, kv_norm, kv_w, kv_knorm, pos_p)
                cmp_s, slc_s, win_s = shared_kv(hs, kv_norm, kv_w, kv_knorm, pos_s)
                kc_p, vc_p = compress_finish(compress_partials(cmp_p, cmp_w1), cmp_w1, cmp_b1, cmp_pe, cmp_w2)
                past_rows = cache_cmp[page_table].reshape(b_s, past_len, 2, N_KV_B, HEAD_DIM)
                n_new = (t_len // CMP_STRIDE) * CMP_STRIDE
                part_s = jnp.concatenate([compress_partials(past_rows, cmp_w1),
                                          compress_partials(cmp_s[:, :n_new], cmp_w1)], axis=1)
                kc_s, vc_s = compress_finish(part_s, cmp_w1, cmp_b1, cmp_pe, cmp_w2)
            j = l - N_LAYERS_A
            q, gates = b_query(hp, b_norm1[j], b_wq[j], b_qnorm[j], pos_p)
            o_c, p_c = cmp_attention(q, kc_p, vc_p, pos_p)
            idx, val = select_blocks(p_c, pos_p, s_len // SLC_BLK)
            o_s = slc_attention_prompt(q, slc_p[:, :, 0], slc_p[:, :, 1], idx, val)
            o_w = band_attention(q, win_p[:, :, 0], win_p[:, :, 1], WIN_B)
            hp = hp + nsa_merge(gates, o_c, o_s, o_w, b_wo[j], hp.dtype)
            q, gates = b_query(hs, b_norm1[j], b_wq[j], b_qnorm[j], pos_s)
            o_c, p_c = cmp_attention(q, kc_s, vc_s, pos_s)
            idx, val = select_blocks(p_c, pos_s, -(-(past_len + t_len) // SLC_BLK))
            kg, vg = slc_gather_decode(cache_slc, slc_s, page_table, idx, past_len)
            o_s = sel_core(q, kg, vg, idx, val, pos_s)
            o_w = window_decode(q, cache_win_b, win_s, WIN_B, past_len)
            hs = hs + nsa_merge(gates, o_c, o_s, o_w, b_wo[j], hs.dtype)
            hp = swiglu(hp, b_norm2[j], b_w13[j], b_w2[j])
            hs = swiglu(hs, b_norm2[j], b_w13[j], b_w2[j])
    swa_a_prompt = jnp.stack(swa_p, axis=0)
    swa_a_sample = jnp.stack(swa_s, axis=0)
    win_b_prompt = win_p[:, -min(WIN_B, s_len):]
    win_b_sample = jnp.concatenate([cache_win_b, win_s], axis=1)[:, -cache_win_b.shape[1]:]
    return (hp, hs, swa_a_prompt, swa_a_sample, cmp_p, cmp_s, slc_p, slc_s, win_b_prompt, win_b_sample)
```

```python
import functools

import jax
import jax.numpy as jnp
from jax import lax
from jax.experimental import pallas as pl
from jax.experimental.pallas import tpu as pltpu

HEAD_DIM = 64
N_KV = 2
ROW_W = 2 * N_KV * HEAD_DIM
N_BRANCH = 3
WIN_A = 128
WIN_B = 512
CMP_LEN = 32
CMP_STRIDE = 16
SLC_BLK = 64
N_SEL = 16
PAGE_SIZE = 128
ROPE_THETA = 500000.0
ROT_DIM = HEAD_DIM // 4
EPS = 1e-6
FORCE_SCORE = 1e9
SCALE = HEAD_DIM ** -0.5
NEG = -1e30

LANES = 128
VMEM_LIMIT = 56 * 1024 * 1024

BF16 = jnp.bfloat16
F32 = jnp.float32


def _params(sem):
    return pltpu.CompilerParams(dimension_semantics=sem, vmem_limit_bytes=VMEM_LIMIT)


def _dot(a, b):
    return jnp.dot(a, b, preferred_element_type=F32)


def _dot_t(a, b):
    return lax.dot_general(a, b, (((1,), (1,)), ((), ())), preferred_element_type=F32)


def _rms(x, g):
    return x * lax.rsqrt(jnp.mean(x * x, axis=-1, keepdims=True) + EPS) * g


def _proj_kernel(x_ref, g_ref, w_ref, hg_ref, bd_ref, c_ref, s1_ref, s2_ref, *out_refs, plan):
    xn = _rms(x_ref[...], g_ref[...]).astype(BF16)
    y = _dot(xn, w_ref[...])
    cos, s1, s2 = c_ref[...], s1_ref[...], s2_ref[...]
    for blk, (kind, out_idx, out_blk, scale) in enumerate(plan):
        lanes = slice(blk * LANES, (blk + 1) * LANES)
        t = y[:, lanes]
        if kind == "rot":
            ms = _dot((t * t).astype(BF16), bd_ref[...])
            t = t * lax.rsqrt(ms + EPS) * hg_ref[:, lanes]
            t = t * cos + pltpu.roll(t, LANES - ROT_DIM // 2, 1) * s1 + pltpu.roll(t, ROT_DIM // 2, 1) * s2
        elif kind == "sig":
            t = jax.nn.sigmoid(t)
        if scale != 1.0:
            t = t * scale
        out_refs[out_idx][:, out_blk * LANES:(out_blk + 1) * LANES] = t


def _proj(x, g, w, hg, tabs, plan, out_widths, tm):
    m, d = x.shape
    n = w.shape[1]
    bd = jnp.kron(jnp.eye(LANES // HEAD_DIM, dtype=F32), jnp.full((HEAD_DIM, HEAD_DIM), 1.0 / HEAD_DIM, F32)).astype(BF16)
    row = lambda i: (i, 0)
    fix = lambda i: (0, 0)
    return pl.pallas_call(
        functools.partial(_proj_kernel, plan=tuple(plan)),
        grid=(m // tm,),
        in_specs=[pl.BlockSpec((tm, d), row), pl.BlockSpec((1, d), fix), pl.BlockSpec((d, n), fix),
                  pl.BlockSpec((1, n), fix), pl.BlockSpec((LANES, LANES), fix),
                  pl.BlockSpec((tm, LANES), row), pl.BlockSpec((tm, LANES), row), pl.BlockSpec((tm, LANES), row)],
        out_specs=[pl.BlockSpec((tm, wd), row) for wd in out_widths],
        out_shape=[jax.ShapeDtypeStruct((m, wd), F32) for wd in out_widths],
        compiler_params=_params(("parallel",)),
    )(x, g.reshape(1, d), w, hg.reshape(1, n), bd, *tabs)


def _mix_kernel(*refs, n_br, n_p, gated):
    pairs = [refs[2 * br:2 * br + 2] for br in range(n_br)]
    rest = refs[2 * n_br:]
    if gated:
        gt_ref, e_ref, w_ref, r_ref, out_ref = rest
        gt = gt_ref[...]
        g_hi = gt.astype(BF16)
        g_lo = (gt - g_hi.astype(F32)).astype(BF16)
    else:
        w_ref, r_ref, out_ref = rest
    in_prompt = pl.program_id(0) < n_p
    o = None
    for br, (p_ref, s_ref) in enumerate(pairs):
        term = jnp.where(in_prompt, p_ref[...], s_ref[...])
        if gated:
            term = term * (_dot(g_hi, e_ref[br]) + _dot(g_lo, e_ref[br]))
        o = term if o is None else o + term
    out_ref[...] = r_ref[...] + _dot(o.astype(BF16), w_ref[...])


def _mix(branches, w, res, tm, gates=None):
    m, d = res.shape
    mp, k = branches[0][0].shape
    n_p = mp // tm
    n_br = len(branches)
    row = lambda i: (i, 0)
    fix = lambda i: (0, 0)
    in_specs, args = [], []
    for o_p, o_s in branches:
        in_specs += [pl.BlockSpec((tm, k), lambda i: (jnp.minimum(i, n_p - 1), 0)),
                     pl.BlockSpec((tm, k), lambda i: (jnp.maximum(i - n_p, 0), 0))]
        args += [o_p, o_s]
    if gates is not None:
        n_heads = k // HEAD_DIM
        expand = (jnp.arange(LANES)[None, :, None] == (jnp.arange(n_br)[:, None, None] * n_heads
                                                       + jnp.arange(k)[None, None, :] // HEAD_DIM)).astype(BF16)
        in_specs += [pl.BlockSpec((tm, LANES), row), pl.BlockSpec((n_br, LANES, k), lambda i: (0, 0, 0))]
        args += [gates, expand]
    in_specs += [pl.BlockSpec((k, d), fix), pl.BlockSpec((tm, d), row)]
    args += [w, res]
    return pl.pallas_call(
        functools.partial(_mix_kernel, n_br=n_br, n_p=n_p, gated=gates is not None),
        grid=(m // tm,),
        in_specs=in_specs,
        out_specs=pl.BlockSpec((tm, d), row),
        out_shape=jax.ShapeDtypeStruct((m, d), F32),
        compiler_params=_params(("parallel",)),
    )(*args)


def _swiglu_kernel(h_ref, g_ref, wg_ref, wu_ref, w2_ref, out_ref, xn_ref, acc_ref):
    f = pl.program_id(1)

    @pl.when(f == 0)
    def _():
        xn_ref[...] = _rms(h_ref[...], g_ref[...]).astype(BF16)
        acc_ref[...] = h_ref[...]

    xn = xn_ref[...]
    gate = _dot(xn, wg_ref[...])
    up = _dot(xn, wu_ref[...])
    acc_ref[...] += _dot((jax.nn.silu(gate) * up).astype(BF16), w2_ref[...])

    @pl.when(f == pl.num_programs(1) - 1)
    def _():
        out_ref[...] = acc_ref[...]


def _swiglu(h, g, w13, w2, tm, tf):
    m, d = h.shape
    dff = w2.shape[0]
    nf = dff // tf
    return pl.pallas_call(
        _swiglu_kernel,
        grid=(m // tm, nf),
        in_specs=[pl.BlockSpec((tm, d), lambda i, f: (i, 0)), pl.BlockSpec((1, d), lambda i, f: (0, 0)),
                  pl.BlockSpec((d, tf), lambda i, f: (0, f)), pl.BlockSpec((d, tf), lambda i, f: (0, nf + f)),
                  pl.BlockSpec((tf, d), lambda i, f: (f, 0))],
        out_specs=pl.BlockSpec((tm, d), lambda i, f: (i, 0)),
        out_shape=jax.ShapeDtypeStruct((m, d), F32),
        scratch_shapes=[pltpu.VMEM((tm, d), BF16), pltpu.VMEM((tm, d), F32)],
        compiler_params=_params(("parallel", "arbitrary")),
    )(h, g.reshape(1, d), w13, w13, w2)


def _rot_tables(pos):
    half = ROT_DIM // 2
    inv = jnp.power(ROPE_THETA, -jnp.arange(half, dtype=F32) * (2.0 / ROT_DIM))
    ang = pos.astype(F32)[:, None] * inv[None, :]
    cos, sin = jnp.cos(ang), jnp.sin(ang)
    m = pos.shape[0]
    zeros = lambda n: jnp.zeros((m, n), F32)
    c = jnp.concatenate([cos, cos, jnp.ones((m, HEAD_DIM - ROT_DIM), F32)], axis=1)
    s1 = jnp.concatenate([-sin, zeros(HEAD_DIM - half)], axis=1)
    s2 = jnp.concatenate([zeros(half), sin, zeros(HEAD_DIM - ROT_DIM)], axis=1)
    reps = LANES // HEAD_DIM
    return tuple(jnp.tile(t, (1, reps)) for t in (c, s1, s2))


def _project_a(h, norm1, wqkv, qn, kn, tabs, tm):
    nq = wqkv.shape[1] - ROW_W
    hg = jnp.concatenate([jnp.tile(qn, nq // HEAD_DIM), jnp.tile(kn, N_KV), jnp.ones((N_KV * HEAD_DIM,), F32)])
    plan = [("rot", 0, c, SCALE) for c in range(nq // LANES)] + [("rot", 1, 0, 1.0), ("id", 1, 1, 1.0)]
    return _proj(h, norm1, wqkv.astype(BF16), hg, tabs, plan, (nq, ROW_W), tm)


def _project_kv(h, kv_norm, kv_w, kv_knorm, tabs, tm):
    hg = jnp.concatenate([jnp.concatenate([jnp.tile(kv_knorm[br], N_KV), jnp.ones((N_KV * HEAD_DIM,), F32)])
                          for br in range(N_BRANCH)])
    plan = []
    for br in range(N_BRANCH):
        plan += [("rot", br, 0, 1.0), ("id", br, 1, 1.0)]
    return _proj(h, kv_norm, kv_w.astype(BF16), hg, tabs, plan, (ROW_W,) * N_BRANCH, tm)


def _project_bq(h, norm1, wq, qn, tabs, tm):
    n_gate = wq.shape[1] % LANES
    nq = wq.shape[1] - n_gate
    wq = jnp.pad(wq, ((0, 0), (0, LANES - n_gate))).astype(BF16)
    hg = jnp.concatenate([jnp.tile(qn, nq // HEAD_DIM), jnp.ones((LANES,), F32)])
    plan = [("rot", 0, c, SCALE) for c in range(nq // LANES)] + [("sig", 1, 0, 1.0)]
    return _proj(h, norm1, wq, hg, tabs, plan, (nq, LANES), tm)


def _swap_halves(x):
    return pltpu.roll(x, HEAD_DIM, x.ndim - 1)


def _stack_heads(q, g):
    t, width = q.shape
    lane = lax.broadcasted_iota(jnp.int32, (t, LANES), 1)
    on_group = (lane >= HEAD_DIM) == (g == 1)
    parts = []
    for h in range(width // HEAD_DIM):
        src = q[:, (h // 2) * LANES:(h // 2 + 1) * LANES]
        own = (lane >= HEAD_DIM) == (h % 2 == 1)
        both = jnp.where(own, src, _swap_halves(src))
        parts.append(jnp.where(on_group, both, 0.0).astype(BF16))
    return jnp.concatenate(parts, axis=0)


def _unstack_heads(o, g):
    gp, t, _ = o.shape
    lane = lax.broadcasted_iota(jnp.int32, (t, LANES), 1)
    blocks = []
    for j in range(gp // 2):
        halves = []
        for h in (2 * j, 2 * j + 1):
            halves.append(jnp.where(g == h % 2, o[h], _swap_halves(o[h])))
        blocks.append(jnp.where(lane < HEAD_DIM, halves[0], halves[1]))
    return jnp.concatenate(blocks, axis=1)


def _flash_kernel(*refs, mode, window, tq, tk, gp, has_sink):
    if mode == "slc":
        q_ref, kv_ref, sel_ref, *rest = refs
    else:
        q_ref, kv_ref, *rest = refs
    if has_sink:
        sink_ref, *rest = rest
    o_ref, m_scr, l_scr, acc_scr = rest
    g = pl.program_id(1)
    q0 = pl.program_id(2) * tq
    qz = _stack_heads(q_ref[...], g)
    m_scr[...] = jnp.full(m_scr.shape, NEG, F32)
    l_scr[...] = jnp.zeros(l_scr.shape, F32)
    acc_scr[...] = jnp.zeros(acc_scr.shape, F32)
    qpos = q0 + lax.broadcasted_iota(jnp.int32, (tq, tk), 0)
    lo = jnp.maximum(q0 - (window - 1), 0) // tk if mode == "band" else 0
    hi = (q0 + tq - 1) // tk + 1

    def body(kt, carry):
        k0 = pl.multiple_of(kt * tk, tk)
        kvt = kv_ref[pl.ds(k0, tk), :].astype(BF16)
        s = _dot_t(qz, kvt[:, :LANES]).reshape(gp, tq, tk)
        diff = qpos - (k0 + lax.broadcasted_iota(jnp.int32, (tq, tk), 1))
        if mode == "band":
            mask = (diff >= 0) & (diff < window)
        else:
            blk = (k0 + lax.broadcasted_iota(jnp.int32, (LANES, tk), 1)) // SLC_BLK
            expand = (blk == lax.broadcasted_iota(jnp.int32, (LANES, tk), 0)).astype(BF16)
            mask = (diff >= 0) & (_dot(sel_ref[...], expand) > 0.5)
        mask = mask[None]
        m_prev = m_scr[...]
        m_new = jnp.maximum(m_prev, jnp.max(jnp.where(mask, s, NEG), axis=-1, keepdims=True))
        alpha = jnp.exp(m_prev - m_new)
        p = jnp.where(mask, jnp.exp(s - m_new), 0.0)
        l_scr[...] = alpha * l_scr[...] + jnp.sum(p, axis=-1, keepdims=True)
        pv = _dot(p.reshape(gp * tq, tk).astype(BF16), kvt[:, LANES:])
        acc_scr[...] = alpha * acc_scr[...] + pv.reshape(gp, tq, LANES)
        m_scr[...] = m_new
        return carry

    lax.fori_loop(lo, hi, body, 0)
    m, l, acc = m_scr[...], l_scr[...], acc_scr[...]
    if has_sink:
        sink = jnp.concatenate([jnp.full((1, tq, 1), sink_ref[g * gp + h], F32) for h in range(gp)], axis=0)
        m_fin = jnp.maximum(m, sink)
        scale = jnp.exp(m - m_fin)
        l = l * scale + jnp.exp(sink - m_fin)
        acc = acc * scale
    o_ref[...] = _unstack_heads(acc / jnp.where(l > 0, l, 1.0), g)


def _flash_prompt(q, rows, batch, mode, window, tq, tk, sel=None, sink=None):
    nq = q.shape[1]
    m = rows.shape[0]
    s_len = m // batch
    nqt = s_len // tq
    gw = nq // N_KV
    gp = gw // HEAD_DIM
    in_specs = [pl.BlockSpec((tq, gw), lambda b, g, i: (b * nqt + i, g)),
                pl.BlockSpec((None, s_len, ROW_W), lambda b, g, i: (b, 0, 0))]
    args = [q, rows.reshape(batch, s_len, ROW_W)]
    if mode == "slc":
        in_specs.append(pl.BlockSpec((None, tq, LANES), lambda b, g, i: (g, b * nqt + i, 0)))
        args.append(sel)
    if sink is not None:
        in_specs.append(pl.BlockSpec(memory_space=pltpu.SMEM))
        args.append(sink)
    return pl.pallas_call(
        functools.partial(_flash_kernel, mode=mode, window=window, tq=tq, tk=tk, gp=gp, has_sink=sink is not None),
        grid=(batch, N_KV, nqt),
        in_specs=in_specs,
        out_specs=pl.BlockSpec((tq, gw), lambda b, g, i: (b * nqt + i, g)),
        out_shape=jax.ShapeDtypeStruct((m, nq), F32),
        scratch_shapes=[pltpu.VMEM((gp, tq, 1), F32), pltpu.VMEM((gp, tq, 1), F32), pltpu.VMEM((gp, tq, LANES), F32)],
        compiler_params=_params(("parallel", "parallel", "arbitrary")),
    )(*args)


def _top_blocks(score, n_sel):
    t, w = score.shape
    lane = lax.broadcasted_iota(jnp.int32, (t, w), 1)
    picked = jnp.zeros((t, w), F32)
    idx = jnp.zeros((t, LANES), jnp.int32)
    idx_lane = lax.broadcasted_iota(jnp.int32, (t, LANES), 1)
    for it in range(n_sel):
        top = jnp.max(score, axis=-1, keepdims=True)
        first = jnp.min(jnp.where(score == top, lane, w), axis=-1, keepdims=True)
        hit = lane == first
        picked = jnp.where(hit, 1.0, picked)
        score = jnp.where(hit, -jnp.inf, score)
        idx = jnp.where(idx_lane == it, first, idx)
    return picked, idx


def _block_scores(psum, overlap_ref, qpos, ns):
    p_hi = psum.astype(BF16)
    p_lo = (psum - p_hi.astype(F32)).astype(BF16)
    imp = _dot(p_hi, overlap_ref[...]) + _dot(p_lo, overlap_ref[...])
    j = lax.broadcasted_iota(jnp.int32, imp.shape, 1)
    cur = qpos // SLC_BLK
    forced = (j == 0) | (j == cur) | (j == cur - 1)
    score = jnp.where(forced, FORCE_SCORE, jnp.where(j <= cur, imp, -1.0))
    return jnp.where(j < ns, score, -jnp.inf)


def _cmp_prompt_kernel(q_ref, kvc_ref, ov_ref, o_ref, sel_ref, *, tq, gp, nc, ns):
    g = pl.program_id(1)
    q0 = pl.program_id(2) * tq
    qz = _stack_heads(q_ref[...], g)
    kvc = kvc_ref[...].astype(BF16)
    ncp = kvc.shape[0]
    s = _dot_t(qz, kvc[:, :LANES]).reshape(gp, tq, ncp)
    qpos = q0 + lax.broadcasted_iota(jnp.int32, (tq, ncp), 0)
    c = lax.broadcasted_iota(jnp.int32, (tq, ncp), 1)
    mask = ((c * CMP_STRIDE + CMP_LEN - 1 <= qpos) & (c < nc))[None]
    m = jnp.max(jnp.where(mask, s, NEG), axis=-1, keepdims=True)
    e = jnp.where(mask, jnp.exp(s - m), 0.0)
    den = jnp.sum(e, axis=-1, keepdims=True)
    p = e / jnp.where(den > 0, den, 1.0)
    o = _dot(p.reshape(gp * tq, ncp).astype(BF16), kvc[:, LANES:]).reshape(gp, tq, LANES)
    o_ref[...] = _unstack_heads(o, g)
    score = _block_scores(jnp.sum(p, axis=0), ov_ref, q0 + lax.broadcasted_iota(jnp.int32, (tq, ov_ref.shape[1]), 0), ns)
    picked, _ = _top_blocks(score, min(N_SEL, ns))
    sel_ref[...] = picked.astype(BF16)


def _to_qz(q, batch):
    m, nq = q.shape
    t_len = m // batch
    gp = nq // (N_KV * HEAD_DIM)
    q5 = q.reshape(batch, t_len, N_KV, gp, HEAD_DIM).transpose(0, 2, 1, 3, 4)
    zero = jnp.zeros_like(q5[:, 0])
    qz = jnp.stack([jnp.concatenate([q5[:, 0], zero], axis=-1), jnp.concatenate([zero, q5[:, 1]], axis=-1)], axis=1)
    return qz.reshape(batch, N_KV, t_len * gp, LANES).astype(BF16)


def _from_qz(o, t_len):
    batch, _, rows, _ = o.shape
    gp = rows // t_len
    o6 = o.reshape(batch, N_KV, t_len, gp, LANES // HEAD_DIM, HEAD_DIM)
    x = jnp.stack([o6[:, g, :, :, g, :] for g in range(N_KV)], axis=2)
    return x.reshape(batch * t_len, N_KV * gp * HEAD_DIM)


def _decode_softmax(s, mask, sink=None):
    m = jnp.max(jnp.where(mask, s, NEG), axis=-1, keepdims=True)
    if sink is not None:
        m = jnp.maximum(m, sink)
    e = jnp.where(mask, jnp.exp(s - m), 0.0)
    den = jnp.sum(e, axis=-1, keepdims=True)
    if sink is not None:
        den = den + jnp.exp(sink - m)
    return e / jnp.where(den > 0, den, 1.0)


def _window_decode_kernel(*refs, bs, t_len, gp, window, has_sink):
    if has_sink:
        qz_ref, buf_ref, new_ref, sink_ref, o_ref = refs
    else:
        qz_ref, buf_ref, new_ref, o_ref = refs
    wb = buf_ref.shape[1]
    pad = (-(wb + t_len)) % LANES
    nk = wb + t_len + pad
    diff = (lax.broadcasted_iota(jnp.int32, (t_len, gp, nk), 0) + wb
            - lax.broadcasted_iota(jnp.int32, (t_len, gp, nk), 2))
    mask = (diff >= 0) & (diff < window)
    for b in range(bs):
        kv = jnp.concatenate([buf_ref[b], new_ref[pl.ds(b * t_len, t_len), :], jnp.zeros((pad, ROW_W), F32)],
                             axis=0).astype(BF16)
        for g in range(N_KV):
            s = _dot_t(qz_ref[b, g], kv[:, :LANES]).reshape(t_len, gp, nk)
            sink = sink_ref[g].reshape(1, gp, 1) if has_sink else None
            p = _decode_softmax(s, mask, sink)
            o_ref[b, g] = _dot(p.reshape(t_len * gp, nk).astype(BF16), kv[:, LANES:])


def _window_decode(qz, buf, new_rows, window, bs, sink=None):
    batch, _, rows, _ = qz.shape
    wb = buf.shape[1]
    t_len = new_rows.shape[0] // batch
    gp = rows // t_len
    in_specs = [pl.BlockSpec((bs, N_KV, rows, LANES), lambda i: (i, 0, 0, 0)),
                pl.BlockSpec((bs, wb, ROW_W), lambda i: (i, 0, 0)),
                pl.BlockSpec((bs * t_len, ROW_W), lambda i: (i, 0))]
    args = [qz, buf, new_rows]
    if sink is not None:
        in_specs.append(pl.BlockSpec((N_KV, gp, 1), lambda i: (0, 0, 0)))
        args.append(sink.reshape(N_KV, gp, 1))
    return pl.pallas_call(
        functools.partial(_window_decode_kernel, bs=bs, t_len=t_len, gp=gp, window=window, has_sink=sink is not None),
        grid=(batch // bs,),
        in_specs=in_specs,
        out_specs=pl.BlockSpec((bs, N_KV, rows, LANES), lambda i: (i, 0, 0, 0)),
        out_shape=jax.ShapeDtypeStruct((batch, N_KV, rows, LANES), F32),
        compiler_params=_params(("parallel",)),
    )(*args)


def _cmp_decode_kernel(qz_ref, kvc_ref, ov_ref, o_ref, idx_ref, *, bs, t_len, gp, nc, ns, past_len):
    ncp = kvc_ref.shape[1]
    qpos = past_len + lax.broadcasted_iota(jnp.int32, (t_len, gp, ncp), 0)
    c = lax.broadcasted_iota(jnp.int32, (t_len, gp, ncp), 2)
    mask = (c * CMP_STRIDE + CMP_LEN - 1 <= qpos) & (c < nc)
    qpos_blk = past_len + lax.broadcasted_iota(jnp.int32, (t_len, ov_ref.shape[1]), 0)
    for b in range(bs):
        kvc = kvc_ref[b].astype(BF16)
        for g in range(N_KV):
            s = _dot_t(qz_ref[b, g], kvc[:, :LANES]).reshape(t_len, gp, ncp)
            p = _decode_softmax(s, mask)
            o_ref[b, g] = _dot(p.reshape(t_len * gp, ncp).astype(BF16), kvc[:, LANES:])
            score = _block_scores(jnp.sum(p, axis=1), ov_ref, qpos_blk, ns)
            _, idx = _top_blocks(score, min(N_SEL, ns))
            idx_ref[b, g] = idx


def _cmp_decode(qz, kvc, nc, ns, past_len, t_len, bs):
    batch, _, rows, _ = qz.shape
    ncp = kvc.shape[1]
    nsp = -(-ns // LANES) * LANES
    return pl.pallas_call(
        functools.partial(_cmp_decode_kernel, bs=bs, t_len=t_len, gp=rows // t_len, nc=nc, ns=ns, past_len=past_len),
        grid=(batch // bs,),
        in_specs=[pl.BlockSpec((bs, N_KV, rows, LANES), lambda i: (i, 0, 0, 0)),
                  pl.BlockSpec((bs, ncp, ROW_W), lambda i: (i, 0, 0)),
                  pl.BlockSpec((ncp, nsp), lambda i: (0, 0))],
        out_specs=[pl.BlockSpec((bs, N_KV, rows, LANES), lambda i: (i, 0, 0, 0)),
                   pl.BlockSpec((bs, N_KV, t_len, LANES), lambda i: (i, 0, 0, 0))],
        out_shape=[jax.ShapeDtypeStruct((batch, N_KV, rows, LANES), F32),
                   jax.ShapeDtypeStruct((batch, N_KV, t_len, LANES), jnp.int32)],
        compiler_params=_params(("parallel",)),
    )(qz, kvc, _overlap(ncp, nsp))


CODE_BITS = 10


def _slc_codes(idx, page_table, rows_new, past_len, n_pool):
    batch, _, t_len, n_sel = idx.shape
    per_page = PAGE_SIZE // SLC_BLK
    past_blocks = past_len // SLC_BLK
    nnb = -(-t_len // SLC_BLK)
    ip = jnp.minimum(idx, past_blocks - 1)
    page = jnp.take_along_axis(page_table, (ip // per_page).reshape(batch, -1), axis=1).reshape(idx.shape)
    src_past = page * per_page + ip % per_page
    src_new = (n_pool * per_page + jnp.arange(batch, dtype=jnp.int32)[:, None, None, None] * nnb
               + jnp.clip(idx - past_blocks, 0, nnb - 1))
    src = jnp.where(idx < past_blocks, src_past, src_new)
    codes = ((src << CODE_BITS) | idx).astype(jnp.int32).reshape(batch, N_KV * t_len * n_sel)
    new = jnp.pad(rows_new.reshape(batch, t_len, ROW_W), ((0, 0), (0, nnb * SLC_BLK - t_len), (0, 0)))
    return codes, new.reshape(batch * nnb, SLC_BLK, ROW_W)


def _slc_decode_kernel(code_ref, qz_ref, pool_ref, newblk_ref, o_ref, kbuf, sem, *, n_prob, t_len, gp, n_sel,
                       n_pool_blocks, past_len):
    b = pl.program_id(0)
    total = pl.num_programs(0) * n_prob

    def source(bb, i, k):
        return code_ref[bb, i * n_sel + k] >> CODE_BITS

    def pool_copy(src, sl, k):
        return pltpu.make_async_copy(pool_ref.at[jnp.minimum(src, n_pool_blocks - 1)],
                                     kbuf.at[sl, pl.ds(k * SLC_BLK, SLC_BLK), :], sem.at[sl])

    def new_copy(src, sl, k):
        return pltpu.make_async_copy(newblk_ref.at[jnp.maximum(src - n_pool_blocks, 0)],
                                     kbuf.at[sl, pl.ds(k * SLC_BLK, SLC_BLK), :], sem.at[sl])

    def start(bb, i, sl):
        def one(k, c):
            src = source(bb, i, k)
            pl.when(src < n_pool_blocks)(lambda: pool_copy(src, sl, k).start())
            pl.when(src >= n_pool_blocks)(lambda: new_copy(src, sl, k).start())
            return c
        lax.fori_loop(0, n_sel, one, 0)

    def wait(bb, i, sl):
        lax.fori_loop(0, n_sel, lambda k, c: (pool_copy(source(bb, i, k), sl, k).wait(), c)[1], 0)

    @pl.when(b == 0)
    def _():
        start(0, 0, 0)

    lane = lax.broadcasted_iota(jnp.int32, (gp, n_sel * SLC_BLK), 1)

    def problem(i, carry):
        n = b * n_prob + i
        sl = n % 2
        wait(b, i, sl)

        @pl.when(n + 1 < total)
        def _():
            start((n + 1) // n_prob, (n + 1) % n_prob, 1 - sl)

        t = i % t_len
        kpos = jnp.zeros((gp, n_sel * SLC_BLK), jnp.int32)
        for k in range(n_sel):
            blk = code_ref[b, i * n_sel + k] & ((1 << CODE_BITS) - 1)
            kpos = jnp.where(lane // SLC_BLK == k, blk * SLC_BLK + lane % SLC_BLK, kpos)
        mask = kpos <= past_len + t
        kv = kbuf[sl].astype(BF16)
        rows = pl.ds(pl.multiple_of(i * gp, gp), gp)
        s = _dot_t(qz_ref[rows, :], kv[:, :LANES])
        p = _decode_softmax(s, mask)
        o_ref[rows, :] = _dot(p.astype(BF16), kv[:, LANES:])
        return carry

    lax.fori_loop(0, n_prob, problem, 0)


def _slc_decode(codes, qz, pool_blocks, new_blocks, t_len, past_len):
    batch, _, rows, _ = qz.shape
    gp = rows // t_len
    n_prob = N_KV * t_len
    n_sel = codes.shape[1] // n_prob
    grid_spec = pltpu.PrefetchScalarGridSpec(
        num_scalar_prefetch=1,
        grid=(batch,),
        in_specs=[pl.BlockSpec((None, N_KV * rows, LANES), lambda b, code: (b, 0, 0)),
                  pl.BlockSpec(memory_space=pl.ANY), pl.BlockSpec(memory_space=pl.ANY)],
        out_specs=pl.BlockSpec((None, N_KV * rows, LANES), lambda b, code: (b, 0, 0)),
        scratch_shapes=[pltpu.VMEM((2, n_sel * SLC_BLK, ROW_W), F32), pltpu.SemaphoreType.DMA((2,))],
    )
    out = pl.pallas_call(
        functools.partial(_slc_decode_kernel, n_prob=n_prob, t_len=t_len, gp=gp, n_sel=n_sel,
                          n_pool_blocks=pool_blocks.shape[0], past_len=past_len),
        grid_spec=grid_spec,
        out_shape=jax.ShapeDtypeStruct((batch, N_KV * rows, LANES), F32),
        compiler_params=_params(("arbitrary",)),
    )(codes, qz.reshape(batch, N_KV * rows, LANES), pool_blocks, new_blocks)
    return out.reshape(batch, N_KV, rows, LANES)


def _compress_kernel(tab_ref, pool_ref, w1p_ref, w1e_ref, pe_ref, b1_ref, w2_ref, out_ref, xbuf, sem, *, pg, nj, npg):
    b, j = pl.program_id(0), pl.program_id(1)
    n = b * nj + j
    slot = n % 2
    page_rows = PAGE_SIZE
    tail0 = pg * page_rows
    n_chunks = (pg + 1) * page_rows // CMP_STRIDE
    c_out = pg * page_rows // CMP_STRIDE

    def page_copy(bb, jj, sl, s, p):
        return pltpu.make_async_copy(pool_ref.at[tab_ref[bb, jj * pg + p], :, pl.ds(s * LANES, LANES)],
                                     xbuf.at[sl, s, pl.ds(p * page_rows, page_rows), :], sem.at[sl])

    def tail_copy(bb, jj, sl, s):
        nxt = tab_ref[bb, jnp.minimum((jj + 1) * pg, npg - 1)]
        return pltpu.make_async_copy(pool_ref.at[nxt, pl.ds(0, CMP_STRIDE), pl.ds(s * LANES, LANES)],
                                     xbuf.at[sl, s, pl.ds(tail0, CMP_STRIDE), :], sem.at[sl])

    def start(bb, jj, sl):
        for s in range(2):
            lax.fori_loop(0, pg, lambda p, c: (page_copy(bb, jj, sl, s, p).start(), c)[1], 0)
            tail_copy(bb, jj, sl, s).start()

    def wait(bb, jj, sl):
        for s in range(2):
            lax.fori_loop(0, pg, lambda p, c: (page_copy(bb, jj, sl, s, p).wait(), c)[1], 0)
            tail_copy(bb, jj, sl, s).wait()

    @pl.when(n == 0)
    def _():
        for sl in range(2):
            for s in range(2):
                xbuf[sl, s, pl.ds(tail0 + CMP_STRIDE, page_rows - CMP_STRIDE), :] = jnp.zeros(
                    (page_rows - CMP_STRIDE, LANES), F32)
        start(b, j, slot)

    wait(b, j, slot)

    @pl.when(n + 1 < pl.num_programs(0) * nj)
    def _():
        nn = n + 1
        start(nn // nj, nn % nj, 1 - slot)

    outs = []
    for s in range(2):
        acc = jnp.zeros((n_chunks, ROW_W), F32)
        for pair in range(CMP_STRIDE // 2):
            x0 = xbuf[slot, s, pl.ds(2 * pair, n_chunks, stride=CMP_STRIDE), :]
            x1 = xbuf[slot, s, pl.ds(2 * pair + 1, n_chunks, stride=CMP_STRIDE), :]
            acc = acc + _dot(jnp.concatenate([x0, x1], axis=1).astype(BF16), w1p_ref[s, pair])
        bias = _dot(pe_ref[s], w1e_ref[s])[0:1] + b1_ref[s]
        hidden = []
        for k in range(N_KV):
            a = acc[:, k * LANES:(k + 1) * LANES]
            nxt = pltpu.roll(_swap_halves(a), n_chunks - 1, 0)
            hidden.append(jax.nn.gelu(a + nxt + bias).astype(BF16))
        outs.append(_dot(jnp.concatenate(hidden, axis=1), w2_ref[s]))
    out_ref[...] = jnp.concatenate(outs, axis=1)[:c_out]


def _compress(pool, table, cmp_w1, cmp_b1, cmp_pe, cmp_w2, pg):
    nb, npg = table.shape
    nj = npg // pg
    hid = cmp_w1.shape[-1]
    w = cmp_w1.reshape(2, CMP_LEN // CMP_STRIDE, CMP_STRIDE, HEAD_DIM, hid)
    base = w.transpose(0, 2, 3, 1, 4).reshape(2, CMP_STRIDE, HEAD_DIM, 2 * hid)
    eye = jnp.eye(N_KV, dtype=F32)
    wsl = jnp.einsum("ab,sldn->sladbn", eye, base).reshape(2, CMP_STRIDE, LANES, ROW_W)
    w1p = wsl.reshape(2, CMP_STRIDE // 2, 2 * LANES, ROW_W).astype(BF16)
    w1e = jnp.pad(cmp_w1, ((0, 0), (0, 0), (0, LANES - hid))).astype(BF16)
    pe = jnp.broadcast_to(cmp_pe.reshape(2, 1, CMP_LEN * HEAD_DIM), (2, 8, CMP_LEN * HEAD_DIM)).astype(BF16)
    b1 = jnp.pad(cmp_b1, ((0, 0), (0, LANES - hid))).reshape(2, 1, LANES)
    w2h = jnp.pad(cmp_w2, ((0, 0), (0, LANES - hid), (0, 0)))
    w2c = jnp.einsum("ab,shd->sahbd", eye, w2h).reshape(2, ROW_W, N_KV * HEAD_DIM).astype(BF16)
    c_out = pg * PAGE_SIZE // CMP_STRIDE
    fix3 = lambda b, j, tab: (0, 0, 0)
    grid_spec = pltpu.PrefetchScalarGridSpec(
        num_scalar_prefetch=1,
        grid=(nb, nj),
        in_specs=[pl.BlockSpec(memory_space=pl.ANY),
                  pl.BlockSpec(w1p.shape, lambda b, j, tab: (0, 0, 0, 0)), pl.BlockSpec(w1e.shape, fix3),
                  pl.BlockSpec(pe.shape, fix3), pl.BlockSpec(b1.shape, fix3), pl.BlockSpec(w2c.shape, fix3)],
        out_specs=pl.BlockSpec((None, c_out, ROW_W), lambda b, j, tab: (b, j, 0)),
        scratch_shapes=[pltpu.VMEM((2, 2, (pg + 1) * PAGE_SIZE, LANES), F32), pltpu.SemaphoreType.DMA((2,))],
    )
    return pl.pallas_call(
        functools.partial(_compress_kernel, pg=pg, nj=nj, npg=npg),
        grid_spec=grid_spec,
        out_shape=jax.ShapeDtypeStruct((nb, npg * PAGE_SIZE // CMP_STRIDE, ROW_W), F32),
        compiler_params=_params(("arbitrary", "arbitrary")),
    )(table, pool, w1p, w1e, pe, b1, w2c)


def _overlap(ncp, nsp):
    ci = jnp.arange(ncp)[:, None] * CMP_STRIDE
    sj = jnp.arange(nsp)[None, :] * SLC_BLK
    return ((ci < sj + SLC_BLK) & (ci + CMP_LEN > sj)).astype(BF16)


def _cmp_prompt(q, kvc, batch, s_len, nc, tq):
    nq = q.shape[1]
    m = batch * s_len
    nqt = s_len // tq
    gw = nq // N_KV
    ncp = kvc.shape[1]
    ns = s_len // SLC_BLK
    return pl.pallas_call(
        functools.partial(_cmp_prompt_kernel, tq=tq, gp=gw // HEAD_DIM, nc=nc, ns=ns),
        grid=(batch, N_KV, nqt),
        in_specs=[pl.BlockSpec((tq, gw), lambda b, g, i: (b * nqt + i, g)),
                  pl.BlockSpec((None, ncp, ROW_W), lambda b, g, i: (b, 0, 0)),
                  pl.BlockSpec((ncp, LANES), lambda b, g, i: (0, 0))],
        out_specs=[pl.BlockSpec((tq, gw), lambda b, g, i: (b * nqt + i, g)),
                   pl.BlockSpec((None, tq, LANES), lambda b, g, i: (g, b * nqt + i, 0))],
        out_shape=[jax.ShapeDtypeStruct((m, nq), F32), jax.ShapeDtypeStruct((N_KV, m, LANES), BF16)],
        compiler_params=_params(("parallel", "parallel", "parallel")),
    )(q, kvc, _overlap(ncp, LANES))


TM = 512
TF = 1408
TQ = 128
DEC_BS = 8
CMP_PG_SAMPLE = 64


def kernel(x_prompt, x_sample, cache_swa_a, cache_cmp, cache_slc, cache_win_b, page_table, a_norm1, a_wqkv, a_qnorm, a_knorm, a_sink, a_wo, a_norm2, a_w13, a_w2, kv_norm, kv_w, kv_knorm, cmp_pe, cmp_w1, cmp_b1, cmp_w2, b_norm1, b_wq, b_qnorm, b_wo, b_norm2, b_w13, b_w2):
    b_p, s_len, d = x_prompt.shape
    b_s, t_len, _ = x_sample.shape
    mp, ms = b_p * s_len, b_s * t_len
    n_pool = cache_cmp.shape[0]
    past_len = page_table.shape[1] * PAGE_SIZE
    assert t_len < CMP_STRIDE, "new tokens must not complete a compression chunk"
    pos = jnp.concatenate([jnp.tile(jnp.arange(s_len, dtype=jnp.int32), b_p),
                           jnp.tile(past_len + jnp.arange(t_len, dtype=jnp.int32), b_s)])
    tabs = _rot_tables(pos)
    h = jnp.concatenate([x_prompt.reshape(mp, d), x_sample.reshape(ms, d)], axis=0)
    row5 = lambda r, b: r.reshape(b, -1, 2, N_KV, HEAD_DIM)

    swa_p, swa_s = [], []
    for l in range(a_norm1.shape[0]):
        q, rows = _project_a(h, a_norm1[l], a_wqkv[l], a_qnorm[l], a_knorm[l], tabs, TM)
        rows_p, rows_s = rows[:mp], rows[mp:]
        o_p = _flash_prompt(q, rows_p, b_p, "band", WIN_A, TQ, 128, sink=a_sink[l])
        wa = cache_swa_a.shape[2]
        o_z = _window_decode(_to_qz(q[mp:], b_s), cache_swa_a[l].reshape(b_s, wa, ROW_W), rows_s, WIN_A, DEC_BS,
                             sink=a_sink[l])
        h = _mix([(o_p, _from_qz(o_z, t_len))], a_wo[l].astype(BF16), h, TM)
        swa_p.append(row5(rows_p, b_p)[:, -min(WIN_A, s_len):])
        swa_s.append(jnp.concatenate([cache_swa_a[l], row5(rows_s, b_s)], axis=1)[:, -wa:])
        h = _swiglu(h, a_norm2[l], a_w13[l].astype(BF16), a_w2[l].astype(BF16), TM, TF)

    cmp_rows, slc_rows, win_rows = _project_kv(h, kv_norm, kv_w, kv_knorm, tabs, TM)
    cmp_p, slc_p, win_p = cmp_rows[:mp], slc_rows[:mp], win_rows[:mp]
    cmp_s, slc_s, win_s = cmp_rows[mp:], slc_rows[mp:], win_rows[mp:]
    pages_p = s_len // PAGE_SIZE
    kvc_p = _compress(cmp_p.reshape(b_p * pages_p, PAGE_SIZE, ROW_W),
                      jnp.arange(b_p * pages_p, dtype=jnp.int32).reshape(b_p, pages_p),
                      cmp_w1, cmp_b1, cmp_pe, cmp_w2, pages_p)
    kvc_s = _compress(cache_cmp.reshape(n_pool, PAGE_SIZE, ROW_W), page_table, cmp_w1, cmp_b1, cmp_pe, cmp_w2,
                      CMP_PG_SAMPLE)
    nc_p = s_len // CMP_STRIDE - CMP_LEN // CMP_STRIDE + 1
    nc_s = past_len // CMP_STRIDE - CMP_LEN // CMP_STRIDE + 1
    ns_s = -(-(past_len + t_len) // SLC_BLK)
    wb = cache_win_b.shape[1]

    for j in range(b_norm1.shape[0]):
        q, gates = _project_bq(h, b_norm1[j], b_wq[j], b_qnorm[j], tabs, TM)
        oc_p, sel = _cmp_prompt(q, kvc_p, b_p, s_len, nc_p, TQ)
        os_p = _flash_prompt(q, slc_p, b_p, "slc", 0, TQ, 256, sel=sel)
        ow_p = _flash_prompt(q, win_p, b_p, "band", WIN_B, TQ, 256)
        qz = _to_qz(q[mp:], b_s)
        oc_z, idx = _cmp_decode(qz, kvc_s, nc_s, ns_s, past_len, t_len, DEC_BS // 2)
        codes, new_blocks = _slc_codes(idx[..., :min(N_SEL, ns_s)], page_table, slc_s, past_len, n_pool)
        os_z = _slc_decode(codes, qz, cache_slc.reshape(n_pool * (PAGE_SIZE // SLC_BLK), SLC_BLK, ROW_W),
                           new_blocks, t_len, past_len)
        ow_z = _window_decode(qz, cache_win_b.reshape(b_s, wb, ROW_W), win_s, WIN_B, DEC_BS)
        h = _mix([(oc_p, _from_qz(oc_z, t_len)), (os_p, _from_qz(os_z, t_len)), (ow_p, _from_qz(ow_z, t_len))],
                 b_wo[j].astype(BF16), h, TM, gates=gates)
        h = _swiglu(h, b_norm2[j], b_w13[j].astype(BF16), b_w2[j].astype(BF16), TM, TF)

    hp = h[:mp].reshape(b_p, s_len, d)
    hs = h[mp:].reshape(b_s, t_len, d)
    win_b_prompt = row5(win_p, b_p)[:, -min(WIN_B, s_len):]
    win_b_sample = jnp.concatenate([cache_win_b, row5(win_s, b_s)], axis=1)[:, -wb:]
    return (hp, hs, jnp.stack(swa_p, axis=0), jnp.stack(swa_s, axis=0),
            row5(cmp_p, b_p), row5(cmp_s, b_s), row5(slc_p, b_p), row5(slc_s, b_s), win_b_prompt, win_b_sample)
```

```python
import functools

import jax
import jax.numpy as jnp
from jax import lax
from jax.experimental import pallas as pl
from jax.experimental.pallas import tpu as pltpu

HEAD_DIM = 64
N_KV = 2
ROW_W = 2 * N_KV * HEAD_DIM
N_BRANCH = 3
WIN_A = 128
WIN_B = 512
CMP_LEN = 32
CMP_STRIDE = 16
SLC_BLK = 64
N_SEL = 16
PAGE_SIZE = 128
ROPE_THETA = 500000.0
ROT_DIM = HEAD_DIM // 4
EPS = 1e-6
FORCE_SCORE = 1e9
SCALE = HEAD_DIM ** -0.5
NEG = -1e30

LANES = 128
VMEM_LIMIT = 56 * 1024 * 1024

BF16 = jnp.bfloat16
F32 = jnp.float32


def _params(sem):
    return pltpu.CompilerParams(dimension_semantics=sem, vmem_limit_bytes=VMEM_LIMIT)


def _dot(a, b):
    return jnp.dot(a, b, preferred_element_type=F32)


def _dot_t(a, b):
    return lax.dot_general(a, b, (((1,), (1,)), ((), ())), preferred_element_type=F32)


def _rms(x, g):
    return x * lax.rsqrt(jnp.mean(x * x, axis=-1, keepdims=True) + EPS) * g


def _proj_kernel(x_ref, g_ref, w_ref, hg_ref, bd_ref, c_ref, s1_ref, s2_ref, *out_refs, plan):
    xn = _rms(x_ref[...], g_ref[...]).astype(BF16)
    y = _dot(xn, w_ref[...])
    cos, s1, s2 = c_ref[...], s1_ref[...], s2_ref[...]
    for blk, (kind, out_idx, out_blk, scale) in enumerate(plan):
        lanes = slice(blk * LANES, (blk + 1) * LANES)
        t = y[:, lanes]
        if kind == "rot":
            ms = _dot((t * t).astype(BF16), bd_ref[...])
            t = t * lax.rsqrt(ms + EPS) * hg_ref[:, lanes]
            t = t * cos + pltpu.roll(t, LANES - ROT_DIM // 2, 1) * s1 + pltpu.roll(t, ROT_DIM // 2, 1) * s2
        elif kind == "sig":
            t = jax.nn.sigmoid(t)
        if scale != 1.0:
            t = t * scale
        out_refs[out_idx][:, out_blk * LANES:(out_blk + 1) * LANES] = t


def _proj(x, g, w, hg, tabs, plan, out_widths, tm):
    m, d = x.shape
    n = w.shape[1]
    bd = jnp.kron(jnp.eye(LANES // HEAD_DIM, dtype=F32), jnp.full((HEAD_DIM, HEAD_DIM), 1.0 / HEAD_DIM, F32)).astype(BF16)
    row = lambda i: (i, 0)
    fix = lambda i: (0, 0)
    return pl.pallas_call(
        functools.partial(_proj_kernel, plan=tuple(plan)),
        grid=(m // tm,),
        in_specs=[pl.BlockSpec((tm, d), row), pl.BlockSpec((1, d), fix), pl.BlockSpec((d, n), fix),
                  pl.BlockSpec((1, n), fix), pl.BlockSpec((LANES, LANES), fix),
                  pl.BlockSpec((tm, LANES), row), pl.BlockSpec((tm, LANES), row), pl.BlockSpec((tm, LANES), row)],
        out_specs=[pl.BlockSpec((tm, wd), row) for wd in out_widths],
        out_shape=[jax.ShapeDtypeStruct((m, wd), F32) for wd in out_widths],
        compiler_params=_params(("parallel",)),
    )(x, g.reshape(1, d), w, hg.reshape(1, n), bd, *tabs)


def _mix_kernel(*refs, n_br, n_p, gated):
    pairs = [refs[2 * br:2 * br + 2] for br in range(n_br)]
    rest = refs[2 * n_br:]
    if gated:
        gt_ref, e_ref, w_ref, r_ref, out_ref = rest
        gt = gt_ref[...]
        g_hi = gt.astype(BF16)
        g_lo = (gt - g_hi.astype(F32)).astype(BF16)
    else:
        w_ref, r_ref, out_ref = rest
    in_prompt = pl.program_id(0) < n_p
    o = None
    for br, (p_ref, s_ref) in enumerate(pairs):
        term = jnp.where(in_prompt, p_ref[...], s_ref[...])
        if gated:
            term = term * (_dot(g_hi, e_ref[br]) + _dot(g_lo, e_ref[br]))
        o = term if o is None else o + term
    out_ref[...] = r_ref[...] + _dot(o.astype(BF16), w_ref[...])


def _mix(branches, w, res, tm, gates=None):
    m, d = res.shape
    mp, k = branches[0][0].shape
    n_p = mp // tm
    n_br = len(branches)
    row = lambda i: (i, 0)
    fix = lambda i: (0, 0)
    in_specs, args = [], []
    for o_p, o_s in branches:
        in_specs += [pl.BlockSpec((tm, k), lambda i: (jnp.minimum(i, n_p - 1), 0)),
                     pl.BlockSpec((tm, k), lambda i: (jnp.maximum(i - n_p, 0), 0))]
        args += [o_p, o_s]
    if gates is not None:
        n_heads = k // HEAD_DIM
        expand = (jnp.arange(LANES)[None, :, None] == (jnp.arange(n_br)[:, None, None] * n_heads
                                                       + jnp.arange(k)[None, None, :] // HEAD_DIM)).astype(BF16)
        in_specs += [pl.BlockSpec((tm, LANES), row), pl.BlockSpec((n_br, LANES, k), lambda i: (0, 0, 0))]
        args += [gates, expand]
    in_specs += [pl.BlockSpec((k, d), fix), pl.BlockSpec((tm, d), row)]
    args += [w, res]
    return pl.pallas_call(
        functools.partial(_mix_kernel, n_br=n_br, n_p=n_p, gated=gates is not None),
        grid=(m // tm,),
        in_specs=in_specs,
        out_specs=pl.BlockSpec((tm, d), row),
        out_shape=jax.ShapeDtypeStruct((m, d), F32),
        compiler_params=_params(("parallel",)),
    )(*args)


def _swiglu_kernel(h_ref, g_ref, wg_ref, wu_ref, w2_ref, out_ref, xn_ref, acc_ref):
    f = pl.program_id(1)

    @pl.when(f == 0)
    def _():
        xn_ref[...] = _rms(h_ref[...], g_ref[...]).astype(BF16)
        acc_ref[...] = h_ref[...]

    xn = xn_ref[...]
    gate = _dot(xn, wg_ref[...])
    up = _dot(xn, wu_ref[...])
    acc_ref[...] += _dot((jax.nn.silu(gate) * up).astype(BF16), w2_ref[...])

    @pl.when(f == pl.num_programs(1) - 1)
    def _():
        out_ref[...] = acc_ref[...]


def _swiglu(h, g, w13, w2, tm, tf):
    m, d = h.shape
    dff = w2.shape[0]
    nf = dff // tf
    return pl.pallas_call(
        _swiglu_kernel,
        grid=(m // tm, nf),
        in_specs=[pl.BlockSpec((tm, d), lambda i, f: (i, 0)), pl.BlockSpec((1, d), lambda i, f: (0, 0)),
                  pl.BlockSpec((d, tf), lambda i, f: (0, f)), pl.BlockSpec((d, tf), lambda i, f: (0, nf + f)),
                  pl.BlockSpec((tf, d), lambda i, f: (f, 0))],
        out_specs=pl.BlockSpec((tm, d), lambda i, f: (i, 0)),
        out_shape=jax.ShapeDtypeStruct((m, d), F32),
        scratch_shapes=[pltpu.VMEM((tm, d), BF16), pltpu.VMEM((tm, d), F32)],
        compiler_params=_params(("parallel", "arbitrary")),
    )(h, g.reshape(1, d), w13, w13, w2)


def _rot_tables(pos):
    half = ROT_DIM // 2
    inv = jnp.power(ROPE_THETA, -jnp.arange(half, dtype=F32) * (2.0 / ROT_DIM))
    ang = pos.astype(F32)[:, None] * inv[None, :]
    cos, sin = jnp.cos(ang), jnp.sin(ang)
    m = pos.shape[0]
    zeros = lambda n: jnp.zeros((m, n), F32)
    c = jnp.concatenate([cos, cos, jnp.ones((m, HEAD_DIM - ROT_DIM), F32)], axis=1)
    s1 = jnp.concatenate([-sin, zeros(HEAD_DIM - half)], axis=1)
    s2 = jnp.concatenate([zeros(half), sin, zeros(HEAD_DIM - ROT_DIM)], axis=1)
    reps = LANES // HEAD_DIM
    return tuple(jnp.tile(t, (1, reps)) for t in (c, s1, s2))


def _project_a(h, norm1, wqkv, qn, kn, tabs, tm):
    nq = wqkv.shape[1] - ROW_W
    hg = jnp.concatenate([jnp.tile(qn, nq // HEAD_DIM), jnp.tile(kn, N_KV), jnp.ones((N_KV * HEAD_DIM,), F32)])
    plan = [("rot", 0, c, SCALE) for c in range(nq // LANES)] + [("rot", 1, 0, 1.0), ("id", 1, 1, 1.0)]
    return _proj(h, norm1, wqkv.astype(BF16), hg, tabs, plan, (nq, ROW_W), tm)


def _project_kv(h, kv_norm, kv_w, kv_knorm, tabs, tm):
    hg = jnp.concatenate([jnp.concatenate([jnp.tile(kv_knorm[br], N_KV), jnp.ones((N_KV * HEAD_DIM,), F32)])
                          for br in range(N_BRANCH)])
    plan = []
    for br in range(N_BRANCH):
        plan += [("rot", br, 0, 1.0), ("id", br, 1, 1.0)]
    return _proj(h, kv_norm, kv_w.astype(BF16), hg, tabs, plan, (ROW_W,) * N_BRANCH, tm)


def _project_bq(h, norm1, wq, qn, tabs, tm):
    n_gate = wq.shape[1] % LANES
    nq = wq.shape[1] - n_gate
    wq = jnp.pad(wq, ((0, 0), (0, LANES - n_gate))).astype(BF16)
    hg = jnp.concatenate([jnp.tile(qn, nq // HEAD_DIM), jnp.ones((LANES,), F32)])
    plan = [("rot", 0, c, SCALE) for c in range(nq // LANES)] + [("sig", 1, 0, 1.0)]
    return _proj(h, norm1, wq, hg, tabs, plan, (nq, LANES), tm)


def _swap_halves(x):
    return pltpu.roll(x, HEAD_DIM, x.ndim - 1)


def _stack_heads(q, g):
    t, width = q.shape
    lane = lax.broadcasted_iota(jnp.int32, (t, LANES), 1)
    on_group = (lane >= HEAD_DIM) == (g == 1)
    parts = []
    for h in range(width // HEAD_DIM):
        src = q[:, (h // 2) * LANES:(h // 2 + 1) * LANES]
        own = (lane >= HEAD_DIM) == (h % 2 == 1)
        both = jnp.where(own, src, _swap_halves(src))
        parts.append(jnp.where(on_group, both, 0.0).astype(BF16))
    return jnp.concatenate(parts, axis=0)


def _unstack_heads(o, g):
    gp, t, _ = o.shape
    lane = lax.broadcasted_iota(jnp.int32, (t, LANES), 1)
    blocks = []
    for j in range(gp // 2):
        halves = []
        for h in (2 * j, 2 * j + 1):
            halves.append(jnp.where(g == h % 2, o[h], _swap_halves(o[h])))
        blocks.append(jnp.where(lane < HEAD_DIM, halves[0], halves[1]))
    return jnp.concatenate(blocks, axis=1)


def _flash_kernel(*refs, mode, window, tq, tk, gp, has_sink):
    if mode == "slc":
        q_ref, kv_ref, sel_ref, expand_ref, *rest = refs
    else:
        q_ref, kv_ref, *rest = refs
    if has_sink:
        sink_ref, *rest = rest
    o_ref, m_scr, l_scr, acc_scr = rest
    g = pl.program_id(1)
    q0 = pl.program_id(2) * tq
    qz = _stack_heads(q_ref[...], g)
    m_scr[...] = jnp.full(m_scr.shape, NEG, F32)
    l_scr[...] = jnp.zeros(l_scr.shape, F32)
    acc_scr[...] = jnp.zeros(acc_scr.shape, F32)
    qpos = q0 + lax.broadcasted_iota(jnp.int32, (tq, tk), 0)
    lo = jnp.maximum(q0 - (window - 1), 0) // tk if mode == "band" else 0
    hi = (q0 + tq - 1) // tk + 1
    reps = tk // LANES

    def body(kt, carry):
        k0 = pl.multiple_of(kt * tk, tk)
        kvt = kv_ref[pl.ds(k0, tk), :].astype(BF16)
        s_all = _dot_t(qz, kvt[:, :LANES])
        diff = qpos - (k0 + lax.broadcasted_iota(jnp.int32, (tq, tk), 1))
        if mode == "band":
            mask = (diff >= 0) & (diff < window)
        else:
            mask = (diff >= 0) & (_dot(sel_ref[...], expand_ref[kt]) > 0.5)
        ps = []
        for h in range(gp):
            s = jnp.where(mask, s_all[h * tq:(h + 1) * tq], NEG)
            m_prev = m_scr[h]
            m_new = jnp.maximum(m_prev, jnp.max(s, axis=-1, keepdims=True))
            alpha = jnp.exp(m_prev - m_new)
            p = jnp.exp(s - jnp.concatenate([m_new] * reps, axis=1))
            part = p[:, :LANES]
            for r in range(1, reps):
                part = part + p[:, r * LANES:(r + 1) * LANES]
            l_scr[h] = alpha * l_scr[h] + part
            acc_scr[h] = alpha * acc_scr[h]
            m_scr[h] = m_new
            ps.append(p.astype(BF16))
        pv = _dot(jnp.concatenate(ps, axis=0), kvt[:, LANES:])
        for h in range(gp):
            acc_scr[h] += pv[h * tq:(h + 1) * tq]
        return carry

    lax.fori_loop(lo, hi, body, 0)
    m = m_scr[...][:, :, :1]
    l = jnp.sum(l_scr[...], axis=-1, keepdims=True)
    acc = acc_scr[...]
    if has_sink:
        sink = jnp.concatenate([jnp.full((1, tq, 1), sink_ref[g * gp + h], F32) for h in range(gp)], axis=0)
        m_fin = jnp.maximum(m, sink)
        scale = jnp.exp(m - m_fin)
        l = l * scale + jnp.exp(sink - m_fin)
        acc = acc * scale
    o_ref[...] = _unstack_heads(acc * (1.0 / l), g)


def _flash_prompt(q, rows, batch, mode, window, tq, tk, sel=None, sink=None):
    nq = q.shape[1]
    m = rows.shape[0]
    s_len = m // batch
    nqt = s_len // tq
    gw = nq // N_KV
    gp = gw // HEAD_DIM
    in_specs = [pl.BlockSpec((tq, gw), lambda b, g, i: (b * nqt + i, g)),
                pl.BlockSpec((None, s_len, ROW_W), lambda b, g, i: (b, 0, 0))]
    args = [q, rows.reshape(batch, s_len, ROW_W)]
    if mode == "slc":
        key_blk = (jnp.arange(s_len) // SLC_BLK).reshape(s_len // tk, 1, tk)
        expand = (key_blk == jnp.arange(LANES)[None, :, None]).astype(BF16)
        in_specs += [pl.BlockSpec((None, tq, LANES), lambda b, g, i: (g, b * nqt + i, 0)),
                     pl.BlockSpec(expand.shape, lambda b, g, i: (0, 0, 0))]
        args += [sel, expand]
    if sink is not None:
        in_specs.append(pl.BlockSpec(memory_space=pltpu.SMEM))
        args.append(sink)
    return pl.pallas_call(
        functools.partial(_flash_kernel, mode=mode, window=window, tq=tq, tk=tk, gp=gp, has_sink=sink is not None),
        grid=(batch, N_KV, nqt),
        in_specs=in_specs,
        out_specs=pl.BlockSpec((tq, gw), lambda b, g, i: (b * nqt + i, g)),
        out_shape=jax.ShapeDtypeStruct((m, nq), F32),
        scratch_shapes=[pltpu.VMEM((gp, tq, LANES), F32)] * 3,
        compiler_params=_params(("parallel", "parallel", "arbitrary")),
    )(*args)


def _top_blocks(score, n_sel):
    t, w = score.shape
    lane = lax.broadcasted_iota(jnp.int32, (t, w), 1)
    picked = jnp.zeros((t, w), F32)
    idx = jnp.zeros((t, LANES), jnp.int32)
    idx_lane = lax.broadcasted_iota(jnp.int32, (t, LANES), 1)
    for it in range(n_sel):
        top = jnp.max(score, axis=-1, keepdims=True)
        first = jnp.min(jnp.where(score == top, lane, w), axis=-1, keepdims=True)
        hit = lane == first
        picked = jnp.where(hit, 1.0, picked)
        score = jnp.where(hit, -jnp.inf, score)
        idx = jnp.where(idx_lane == it, first, idx)
    return picked, idx


def _pick_by_rank(score, n_sel, ns):
    lane = lax.broadcasted_iota(jnp.int32, score.shape, 1)
    beaten = jnp.zeros(score.shape, jnp.int32)
    for j in range(ns):
        col = score[:, j:j + 1]
        beaten = beaten + ((col > score) | ((col == score) & (lane > j))).astype(jnp.int32)
    return jnp.where((beaten < n_sel) & (lane < ns), 1.0, 0.0)


def _block_scores(psum, overlap_ref, qpos, ns):
    p_hi = psum.astype(BF16)
    p_lo = (psum - p_hi.astype(F32)).astype(BF16)
    imp = _dot(p_hi, overlap_ref[...]) + _dot(p_lo, overlap_ref[...])
    j = lax.broadcasted_iota(jnp.int32, imp.shape, 1)
    cur = qpos // SLC_BLK
    forced = (j == 0) | (j == cur) | (j == cur - 1)
    score = jnp.where(forced, FORCE_SCORE, jnp.where(j <= cur, imp, -1.0))
    return jnp.where(j < ns, score, -jnp.inf)


def _cmp_prompt_kernel(q_ref, kvc_ref, ov_ref, o_ref, sel_ref, *, tq, gp, nc, ns):
    g = pl.program_id(1)
    q0 = pl.program_id(2) * tq
    qz = _stack_heads(q_ref[...], g)
    kvc = kvc_ref[...].astype(BF16)
    ncp = kvc.shape[0]
    s_all = _dot_t(qz, kvc[:, :LANES])
    qpos = q0 + lax.broadcasted_iota(jnp.int32, (tq, ncp), 0)
    c = lax.broadcasted_iota(jnp.int32, (tq, ncp), 1)
    mask = (c * CMP_STRIDE + CMP_LEN - 1 <= qpos) & (c < nc)
    ps, psum = [], None
    for h in range(gp):
        p = _decode_softmax(s_all[h * tq:(h + 1) * tq], mask)
        psum = p if psum is None else psum + p
        ps.append(p.astype(BF16))
    o = _dot(jnp.concatenate(ps, axis=0), kvc[:, LANES:]).reshape(gp, tq, LANES)
    o_ref[...] = _unstack_heads(o, g)
    score = _block_scores(psum, ov_ref, q0 + lax.broadcasted_iota(jnp.int32, (tq, ov_ref.shape[1]), 0), ns)
    sel_ref[...] = _pick_by_rank(score, min(N_SEL, ns), ns).astype(BF16)


def _to_qz(q, batch):
    m, nq = q.shape
    t_len = m // batch
    gp = nq // (N_KV * HEAD_DIM)
    q5 = q.reshape(batch, t_len, N_KV, gp, HEAD_DIM).transpose(0, 2, 1, 3, 4)
    zero = jnp.zeros_like(q5[:, 0])
    qz = jnp.stack([jnp.concatenate([q5[:, 0], zero], axis=-1), jnp.concatenate([zero, q5[:, 1]], axis=-1)], axis=1)
    return qz.reshape(batch, N_KV, t_len * gp, LANES).astype(BF16)


def _from_qz(o, t_len):
    batch, _, rows, _ = o.shape
    gp = rows // t_len
    o6 = o.reshape(batch, N_KV, t_len, gp, LANES // HEAD_DIM, HEAD_DIM)
    x = jnp.stack([o6[:, g, :, :, g, :] for g in range(N_KV)], axis=2)
    return x.reshape(batch * t_len, N_KV * gp * HEAD_DIM)


def _decode_softmax(s, mask, sink=None):
    m = jnp.max(jnp.where(mask, s, NEG), axis=-1, keepdims=True)
    if sink is not None:
        m = jnp.maximum(m, sink)
    e = jnp.where(mask, jnp.exp(s - m), 0.0)
    den = jnp.sum(e, axis=-1, keepdims=True)
    if sink is not None:
        den = den + jnp.exp(sink - m)
    return e * (1.0 / jnp.where(den > 0, den, 1.0))


def _tiles(cache):
    nd = cache.ndim
    return jnp.transpose(cache, tuple(range(nd - 4)) + (nd - 3, nd - 2, nd - 1, nd - 4))


def _new_tiles(rows_new, batch):
    t_len = rows_new.shape[0] // batch
    t5 = _tiles(rows_new.reshape(batch, t_len, 2, N_KV, HEAD_DIM))
    return jnp.pad(t5, ((0, 0),) * 4 + ((0, LANES - t_len),))


def _to_qh(q, batch):
    m, nq = q.shape
    t_len = m // batch
    gp = nq // (N_KV * HEAD_DIM)
    return q.reshape(batch, t_len, N_KV, gp, HEAD_DIM).transpose(0, 2, 1, 3, 4).reshape(batch, N_KV, t_len * gp, HEAD_DIM)


def _from_qh(o, t_len):
    batch, _, rows, _ = o.shape
    gp = rows // t_len
    x = o.reshape(batch, N_KV, t_len, gp, HEAD_DIM).transpose(0, 2, 1, 3, 4)
    return x.reshape(batch * t_len, N_KV * gp * HEAD_DIM)


def _window_decode_kernel(*refs, bs, t_len, gp, window, has_sink):
    if has_sink:
        q_ref, buf_ref, new_ref, sink_ref, o_ref = refs
    else:
        q_ref, buf_ref, new_ref, o_ref = refs
    wb = buf_ref.shape[-1]
    nk = wb + LANES
    diff = (lax.broadcasted_iota(jnp.int32, (t_len, gp, nk), 0) + wb
            - lax.broadcasted_iota(jnp.int32, (t_len, gp, nk), 2))
    mask = (diff >= 0) & (diff < window)
    for b in range(bs):
        for g in range(N_KV):
            kt = jnp.concatenate([buf_ref[b, 0, g], new_ref[b, 0, g]], axis=1).astype(BF16)
            vt = jnp.concatenate([buf_ref[b, 1, g], new_ref[b, 1, g]], axis=1).astype(BF16)
            s = _dot(q_ref[b, g].astype(BF16), kt).reshape(t_len, gp, nk)
            sink = sink_ref[g].reshape(1, gp, 1) if has_sink else None
            p = _decode_softmax(s, mask, sink)
            o_ref[b, g] = _dot_t(p.reshape(t_len * gp, nk).astype(BF16), vt)


def _window_decode(qh, buf_t, new_t, window, t_len, bs, sink=None):
    batch, _, rows, _ = qh.shape
    wb = buf_t.shape[-1]
    gp = rows // t_len
    in_specs = [pl.BlockSpec((bs, N_KV, rows, HEAD_DIM), lambda i: (i, 0, 0, 0)),
                pl.BlockSpec((bs, 2, N_KV, HEAD_DIM, wb), lambda i: (i, 0, 0, 0, 0)),
                pl.BlockSpec((bs, 2, N_KV, HEAD_DIM, LANES), lambda i: (i, 0, 0, 0, 0))]
    args = [qh, buf_t, new_t]
    if sink is not None:
        in_specs.append(pl.BlockSpec((N_KV, gp, 1), lambda i: (0, 0, 0)))
        args.append(sink.reshape(N_KV, gp, 1))
    return pl.pallas_call(
        functools.partial(_window_decode_kernel, bs=bs, t_len=t_len, gp=gp, window=window, has_sink=sink is not None),
        grid=(batch // bs,),
        in_specs=in_specs,
        out_specs=pl.BlockSpec((bs, N_KV, rows, HEAD_DIM), lambda i: (i, 0, 0, 0)),
        out_shape=jax.ShapeDtypeStruct((batch, N_KV, rows, HEAD_DIM), F32),
        compiler_params=_params(("parallel",)),
    )(*args)


def _cmp_decode_kernel(qz_ref, kvc_ref, ov_ref, o_ref, idx_ref, *, bs, t_len, gp, nc, ns, past_len):
    ncp = kvc_ref.shape[1]
    qpos = past_len + lax.broadcasted_iota(jnp.int32, (t_len, gp, ncp), 0)
    c = lax.broadcasted_iota(jnp.int32, (t_len, gp, ncp), 2)
    mask = (c * CMP_STRIDE + CMP_LEN - 1 <= qpos) & (c < nc)
    qpos_blk = past_len + lax.broadcasted_iota(jnp.int32, (t_len, ov_ref.shape[1]), 0)
    scores = []
    for b in range(bs):
        kvc = kvc_ref[b].astype(BF16)
        for g in range(N_KV):
            s = _dot_t(qz_ref[b, g], kvc[:, :LANES]).reshape(t_len, gp, ncp)
            p = _decode_softmax(s, mask)
            o_ref[b, g] = _dot(p.reshape(t_len * gp, ncp).astype(BF16), kvc[:, LANES:])
            scores.append(_block_scores(jnp.sum(p, axis=1), ov_ref, qpos_blk, ns))
    _, idx = _top_blocks(jnp.concatenate(scores, axis=0), min(N_SEL, ns))
    for b in range(bs):
        for g in range(N_KV):
            i = b * N_KV + g
            idx_ref[b, g] = idx[i * t_len:(i + 1) * t_len]


def _cmp_decode(qz, kvc, nc, ns, past_len, t_len, bs):
    batch, _, rows, _ = qz.shape
    ncp = kvc.shape[1]
    nsp = -(-ns // LANES) * LANES
    return pl.pallas_call(
        functools.partial(_cmp_decode_kernel, bs=bs, t_len=t_len, gp=rows // t_len, nc=nc, ns=ns, past_len=past_len),
        grid=(batch // bs,),
        in_specs=[pl.BlockSpec((bs, N_KV, rows, LANES), lambda i: (i, 0, 0, 0)),
                  pl.BlockSpec((bs, ncp, ROW_W), lambda i: (i, 0, 0)),
                  pl.BlockSpec((ncp, nsp), lambda i: (0, 0))],
        out_specs=[pl.BlockSpec((bs, N_KV, rows, LANES), lambda i: (i, 0, 0, 0)),
                   pl.BlockSpec((bs, N_KV, t_len, LANES), lambda i: (i, 0, 0, 0))],
        out_shape=[jax.ShapeDtypeStruct((batch, N_KV, rows, LANES), F32),
                   jax.ShapeDtypeStruct((batch, N_KV, t_len, LANES), jnp.int32)],
        compiler_params=_params(("parallel",)),
    )(qz, kvc, _overlap(ncp, nsp))


CODE_BITS = 10


def _slc_codes(idx, page_table, rows_new, past_len, n_pool):
    batch, _, t_len, n_sel = idx.shape
    assert t_len <= PAGE_SIZE, "new rows must fit one page-sized tile"
    per_page = PAGE_SIZE // SLC_BLK
    past_blocks = past_len // SLC_BLK
    ip = jnp.minimum(idx, past_blocks - 1)
    page = jnp.take_along_axis(page_table, (ip // per_page).reshape(batch, -1), axis=1).reshape(idx.shape)
    src_new = n_pool + jnp.arange(batch, dtype=jnp.int32)[:, None, None, None]
    src = jnp.where(idx < past_blocks, page, src_new)
    codes = ((src << CODE_BITS) | idx).astype(jnp.int32).reshape(batch, N_KV * t_len * n_sel)
    return codes, _new_tiles(rows_new, batch).reshape(batch * 2 * N_KV, HEAD_DIM, LANES)


def _slc_decode_kernel(code_ref, q_ref, pool_ref, new_ref, o_ref, kbuf, sem, *, n_prob, t_len, gp, n_sel,
                       n_pool, past_len):
    b = pl.program_id(0)
    total = pl.num_programs(0) * n_prob
    tiles_per_page = 2 * N_KV

    def source(bb, i, k):
        return code_ref[bb, i * n_sel + k] >> CODE_BITS

    def tile_copy(ref, page, g, sl, k, kv):
        return pltpu.make_async_copy(ref.at[page * tiles_per_page + kv * N_KV + g],
                                     kbuf.at[sl, kv, :, pl.ds(k * PAGE_SIZE, PAGE_SIZE)], sem.at[sl])

    def start(bb, i, sl):
        g = i // t_len

        def one(k, c):
            src = source(bb, i, k)
            for kv in range(2):
                pl.when(src < n_pool)(lambda: tile_copy(pool_ref, jnp.minimum(src, n_pool - 1), g, sl, k, kv).start())
                pl.when(src >= n_pool)(lambda: tile_copy(new_ref, jnp.maximum(src - n_pool, 0), g, sl, k, kv).start())
            return c
        lax.fori_loop(0, n_sel, one, 0)

    def wait(bb, i, sl):
        def one(k, c):
            for kv in range(2):
                tile_copy(pool_ref, 0, 0, sl, k, kv).wait()
            return c
        lax.fori_loop(0, n_sel, one, 0)

    @pl.when(b == 0)
    def _():
        start(0, 0, 0)

    nk = n_sel * PAGE_SIZE
    lane = lax.broadcasted_iota(jnp.int32, (gp, nk), 1)
    row = lane % PAGE_SIZE

    def problem(i, carry):
        n = b * n_prob + i
        sl = n % 2
        wait(b, i, sl)

        @pl.when(n + 1 < total)
        def _():
            start((n + 1) // n_prob, (n + 1) % n_prob, 1 - sl)

        t = i % t_len
        blk = jnp.zeros((gp, nk), jnp.int32)
        for k in range(n_sel):
            blk = jnp.where(lane // PAGE_SIZE == k, code_ref[b, i * n_sel + k] & ((1 << CODE_BITS) - 1), blk)
        kpos = (blk // (PAGE_SIZE // SLC_BLK)) * PAGE_SIZE + row
        mask = (kpos // SLC_BLK == blk) & (kpos <= past_len + t)
        rows = pl.ds(pl.multiple_of(i * gp, gp), gp)
        s = _dot(q_ref[rows, :].astype(BF16), kbuf[sl, 0].astype(BF16))
        p = _decode_softmax(s, mask)
        o_ref[rows, :] = _dot_t(p.astype(BF16), kbuf[sl, 1].astype(BF16))
        return carry

    lax.fori_loop(0, n_prob, problem, 0)


def _slc_decode(codes, qh, pool_tiles, new_tiles, t_len, past_len):
    batch, _, rows, _ = qh.shape
    gp = rows // t_len
    n_prob = N_KV * t_len
    n_sel = codes.shape[1] // n_prob
    grid_spec = pltpu.PrefetchScalarGridSpec(
        num_scalar_prefetch=1,
        grid=(batch,),
        in_specs=[pl.BlockSpec((None, N_KV * rows, HEAD_DIM), lambda b, code: (b, 0, 0)),
                  pl.BlockSpec(memory_space=pl.ANY), pl.BlockSpec(memory_space=pl.ANY)],
        out_specs=pl.BlockSpec((None, N_KV * rows, HEAD_DIM), lambda b, code: (b, 0, 0)),
        scratch_shapes=[pltpu.VMEM((2, 2, HEAD_DIM, n_sel * PAGE_SIZE), F32), pltpu.SemaphoreType.DMA((2,))],
    )
    out = pl.pallas_call(
        functools.partial(_slc_decode_kernel, n_prob=n_prob, t_len=t_len, gp=gp, n_sel=n_sel,
                          n_pool=pool_tiles.shape[0] // (2 * N_KV), past_len=past_len),
        grid_spec=grid_spec,
        out_shape=jax.ShapeDtypeStruct((batch, N_KV * rows, HEAD_DIM), F32),
        compiler_params=_params(("arbitrary",)),
    )(codes, qh.reshape(batch, N_KV * rows, HEAD_DIM), pool_tiles, new_tiles)
    return out.reshape(batch, N_KV, rows, HEAD_DIM)


def _compress_kernel(tab_ref, pool_ref, perm_ref, w1p_ref, w1e_ref, pe_ref, b1_ref, w2_ref, out_ref, xt, xl, sem,
                     *, pg, nj, npg):
    b, j = pl.program_id(0), pl.program_id(1)
    n = b * nj + j
    slot = n % 2
    per_page = PAGE_SIZE // CMP_STRIDE
    n_chunks = (pg + 1) * per_page
    c_out = pg * per_page

    def page_copy(bb, jj, sl, p):
        page = tab_ref[bb, jnp.minimum(jj * pg + p, npg - 1)]
        return pltpu.make_async_copy(pool_ref.at[page], xt.at[sl, p], sem.at[sl])

    def start(bb, jj, sl):
        lax.fori_loop(0, pg + 1, lambda p, c: (page_copy(bb, jj, sl, p).start(), c)[1], 0)

    def wait(bb, jj, sl):
        lax.fori_loop(0, pg + 1, lambda p, c: (page_copy(bb, jj, sl, p).wait(), c)[1], 0)

    @pl.when(n == 0)
    def _():
        start(b, j, slot)

    wait(b, j, slot)

    @pl.when(n + 1 < pl.num_programs(0) * nj)
    def _():
        nn = n + 1
        start(nn // nj, nn % nj, 1 - slot)

    def to_rows(p, c):
        t = _dot_t(perm_ref[...], xt[slot, p].astype(BF16))
        for l in range(CMP_STRIDE):
            xl[l, pl.ds(pl.multiple_of(p * per_page, per_page), per_page), :] = t[l * per_page:(l + 1) * per_page, :]
        return c

    lax.fori_loop(0, pg + 1, to_rows, 0, unroll=4)

    outs = []
    for s in range(2):
        acc = jnp.zeros((n_chunks, ROW_W), F32)
        for pair in range(CMP_STRIDE // 2):
            x0 = xl[2 * pair, :, s * LANES:(s + 1) * LANES]
            x1 = xl[2 * pair + 1, :, s * LANES:(s + 1) * LANES]
            acc = acc + _dot(jnp.concatenate([x0, x1], axis=1).astype(BF16), w1p_ref[s, pair])
        bias = _dot(pe_ref[s], w1e_ref[s])[0:1] + b1_ref[s]
        hidden = []
        for k in range(N_KV):
            a = acc[:, k * LANES:(k + 1) * LANES]
            nxt = pltpu.roll(_swap_halves(a), n_chunks - 1, 0)
            hidden.append(jax.nn.gelu(a + nxt + bias).astype(BF16))
        outs.append(_dot(jnp.concatenate(hidden, axis=1), w2_ref[s]))
    out_ref[...] = jnp.concatenate(outs, axis=1)[:c_out]


def _compress(pool, table, cmp_w1, cmp_b1, cmp_pe, cmp_w2, pg):
    nb, npg = table.shape
    per_page = PAGE_SIZE // CMP_STRIDE
    r = jnp.arange(PAGE_SIZE)
    perm = (r[None, :] == (r % per_page)[:, None] * CMP_STRIDE + (r // per_page)[:, None]).astype(BF16)
    nj = npg // pg
    hid = cmp_w1.shape[-1]
    w = cmp_w1.reshape(2, CMP_LEN // CMP_STRIDE, CMP_STRIDE, HEAD_DIM, hid)
    base = w.transpose(0, 2, 3, 1, 4).reshape(2, CMP_STRIDE, HEAD_DIM, 2 * hid)
    eye = jnp.eye(N_KV, dtype=F32)
    wsl = jnp.einsum("ab,sldn->sladbn", eye, base).reshape(2, CMP_STRIDE, LANES, ROW_W)
    w1p = wsl.reshape(2, CMP_STRIDE // 2, 2 * LANES, ROW_W).astype(BF16)
    w1e = jnp.pad(cmp_w1, ((0, 0), (0, 0), (0, LANES - hid))).astype(BF16)
    pe = jnp.broadcast_to(cmp_pe.reshape(2, 1, CMP_LEN * HEAD_DIM), (2, 8, CMP_LEN * HEAD_DIM)).astype(BF16)
    b1 = jnp.pad(cmp_b1, ((0, 0), (0, LANES - hid))).reshape(2, 1, LANES)
    w2h = jnp.pad(cmp_w2, ((0, 0), (0, LANES - hid), (0, 0)))
    w2c = jnp.einsum("ab,shd->sahbd", eye, w2h).reshape(2, ROW_W, N_KV * HEAD_DIM).astype(BF16)
    c_out = pg * PAGE_SIZE // CMP_STRIDE
    fix3 = lambda b, j, tab: (0, 0, 0)
    grid_spec = pltpu.PrefetchScalarGridSpec(
        num_scalar_prefetch=1,
        grid=(nb, nj),
        in_specs=[pl.BlockSpec(memory_space=pl.ANY), pl.BlockSpec(perm.shape, lambda b, j, tab: (0, 0)),
                  pl.BlockSpec(w1p.shape, lambda b, j, tab: (0, 0, 0, 0)), pl.BlockSpec(w1e.shape, fix3),
                  pl.BlockSpec(pe.shape, fix3), pl.BlockSpec(b1.shape, fix3), pl.BlockSpec(w2c.shape, fix3)],
        out_specs=pl.BlockSpec((None, c_out, ROW_W), lambda b, j, tab: (b, j, 0)),
        scratch_shapes=[pltpu.VMEM((2, pg + 1, ROW_W, PAGE_SIZE), F32),
                        pltpu.VMEM((CMP_STRIDE, (pg + 1) * per_page, ROW_W), F32),
                        pltpu.SemaphoreType.DMA((2,))],
    )
    return pl.pallas_call(
        functools.partial(_compress_kernel, pg=pg, nj=nj, npg=npg),
        grid_spec=grid_spec,
        out_shape=jax.ShapeDtypeStruct((nb, npg * per_page, ROW_W), F32),
        compiler_params=_params(("arbitrary", "arbitrary")),
    )(table, pool, perm, w1p, w1e, pe, b1, w2c)


def _overlap(ncp, nsp):
    ci = jnp.arange(ncp)[:, None] * CMP_STRIDE
    sj = jnp.arange(nsp)[None, :] * SLC_BLK
    return ((ci < sj + SLC_BLK) & (ci + CMP_LEN > sj)).astype(BF16)


def _cmp_prompt(q, kvc, batch, s_len, nc, tq):
    nq = q.shape[1]
    m = batch * s_len
    nqt = s_len // tq
    gw = nq // N_KV
    ncp = kvc.shape[1]
    ns = s_len // SLC_BLK
    return pl.pallas_call(
        functools.partial(_cmp_prompt_kernel, tq=tq, gp=gw // HEAD_DIM, nc=nc, ns=ns),
        grid=(batch, N_KV, nqt),
        in_specs=[pl.BlockSpec((tq, gw), lambda b, g, i: (b * nqt + i, g)),
                  pl.BlockSpec((None, ncp, ROW_W), lambda b, g, i: (b, 0, 0)),
                  pl.BlockSpec((ncp, LANES), lambda b, g, i: (0, 0))],
        out_specs=[pl.BlockSpec((tq, gw), lambda b, g, i: (b * nqt + i, g)),
                   pl.BlockSpec((None, tq, LANES), lambda b, g, i: (g, b * nqt + i, 0))],
        out_shape=[jax.ShapeDtypeStruct((m, nq), F32), jax.ShapeDtypeStruct((N_KV, m, LANES), BF16)],
        compiler_params=_params(("parallel", "parallel", "parallel")),
    )(q, kvc, _overlap(ncp, LANES))


TM = 512
TF = 1408
TQ = 128
DEC_BS = 8
CMP_PG_SAMPLE = 64


def kernel(x_prompt, x_sample, cache_swa_a, cache_cmp, cache_slc, cache_win_b, page_table, a_norm1, a_wqkv, a_qnorm, a_knorm, a_sink, a_wo, a_norm2, a_w13, a_w2, kv_norm, kv_w, kv_knorm, cmp_pe, cmp_w1, cmp_b1, cmp_w2, b_norm1, b_wq, b_qnorm, b_wo, b_norm2, b_w13, b_w2):
    b_p, s_len, d = x_prompt.shape
    b_s, t_len, _ = x_sample.shape
    mp, ms = b_p * s_len, b_s * t_len
    n_pool = cache_cmp.shape[0]
    past_len = page_table.shape[1] * PAGE_SIZE
    assert t_len < CMP_STRIDE, "new tokens must not complete a compression chunk"
    pos = jnp.concatenate([jnp.tile(jnp.arange(s_len, dtype=jnp.int32), b_p),
                           jnp.tile(past_len + jnp.arange(t_len, dtype=jnp.int32), b_s)])
    tabs = _rot_tables(pos)
    h = jnp.concatenate([x_prompt.reshape(mp, d), x_sample.reshape(ms, d)], axis=0)
    row5 = lambda r, b: r.reshape(b, -1, 2, N_KV, HEAD_DIM)

    swa_p, swa_s = [], []
    for l in range(a_norm1.shape[0]):
        q, rows = _project_a(h, a_norm1[l], a_wqkv[l], a_qnorm[l], a_knorm[l], tabs, TM)
        rows_p, rows_s = rows[:mp], rows[mp:]
        o_p = _flash_prompt(q, rows_p, b_p, "band", WIN_A, TQ, 128, sink=a_sink[l])
        wa = cache_swa_a.shape[2]
        o_h = _window_decode(_to_qh(q[mp:], b_s), _tiles(cache_swa_a[l]), _new_tiles(rows_s, b_s), WIN_A, t_len,
                             DEC_BS, sink=a_sink[l])
        h = _mix([(o_p, _from_qh(o_h, t_len))], a_wo[l].astype(BF16), h, TM)
        swa_p.append(row5(rows_p, b_p)[:, -min(WIN_A, s_len):])
        swa_s.append(jnp.concatenate([cache_swa_a[l], row5(rows_s, b_s)], axis=1)[:, -wa:])
        h = _swiglu(h, a_norm2[l], a_w13[l].astype(BF16), a_w2[l].astype(BF16), TM, TF)

    cmp_rows, slc_rows, win_rows = _project_kv(h, kv_norm, kv_w, kv_knorm, tabs, TM)
    cmp_p, slc_p, win_p = cmp_rows[:mp], slc_rows[:mp], win_rows[:mp]
    cmp_s, slc_s, win_s = cmp_rows[mp:], slc_rows[mp:], win_rows[mp:]
    pages_p = s_len // PAGE_SIZE
    kvc_p = _compress(cmp_p.reshape(b_p * pages_p, PAGE_SIZE, ROW_W).transpose(0, 2, 1),
                      jnp.arange(b_p * pages_p, dtype=jnp.int32).reshape(b_p, pages_p),
                      cmp_w1, cmp_b1, cmp_pe, cmp_w2, pages_p)
    kvc_s = _compress(_tiles(cache_cmp).reshape(n_pool, ROW_W, PAGE_SIZE), page_table, cmp_w1, cmp_b1, cmp_pe,
                      cmp_w2, CMP_PG_SAMPLE)
    nc_p = s_len // CMP_STRIDE - CMP_LEN // CMP_STRIDE + 1
    nc_s = past_len // CMP_STRIDE - CMP_LEN // CMP_STRIDE + 1
    ns_s = -(-(past_len + t_len) // SLC_BLK)
    wb = cache_win_b.shape[1]

    for j in range(b_norm1.shape[0]):
        q, gates = _project_bq(h, b_norm1[j], b_wq[j], b_qnorm[j], tabs, TM)
        oc_p, sel = _cmp_prompt(q, kvc_p, b_p, s_len, nc_p, TQ)
        os_p = _flash_prompt(q, slc_p, b_p, "slc", 0, TQ, 256, sel=sel)
        ow_p = _flash_prompt(q, win_p, b_p, "band", WIN_B, TQ, 256)
        qh = _to_qh(q[mp:], b_s)
        oc_z, idx = _cmp_decode(_to_qz(q[mp:], b_s), kvc_s, nc_s, ns_s, past_len, t_len, DEC_BS // 2)
        codes, new_tiles = _slc_codes(idx[..., :min(N_SEL, ns_s)], page_table, slc_s, past_len, n_pool)
        os_h = _slc_decode(codes, qh, _tiles(cache_slc).reshape(n_pool * 2 * N_KV, HEAD_DIM, PAGE_SIZE), new_tiles,
                           t_len, past_len)
        ow_h = _window_decode(qh, _tiles(cache_win_b), _new_tiles(win_s, b_s), WIN_B, t_len, DEC_BS)
        h = _mix([(oc_p, _from_qz(oc_z, t_len)), (os_p, _from_qh(os_h, t_len)), (ow_p, _from_qh(ow_h, t_len))],
                 b_wo[j].astype(BF16), h, TM, gates=gates)
        h = _swiglu(h, b_norm2[j], b_w13[j].astype(BF16), b_w2[j].astype(BF16), TM, TF)

    hp = h[:mp].reshape(b_p, s_len, d)
    hs = h[mp:].reshape(b_s, t_len, d)
    win_b_prompt = row5(win_p, b_p)[:, -min(WIN_B, s_len):]
    win_b_sample = jnp.concatenate([cache_win_b, row5(win_s, b_s)], axis=1)[:, -wb:]
    return (hp, hs, jnp.stack(swa_p, axis=0), jnp.stack(swa_s, axis=0),
            row5(cmp_p, b_p), row5(cmp_s, b_s), row5(slc_p, b_p), row5(slc_s, b_s), win_b_prompt, win_b_sample)
```

```python
import functools

import jax
import jax.numpy as jnp
from jax import lax
from jax.experimental import pallas as pl
from jax.experimental.pallas import tpu as pltpu

HEAD_DIM = 64
N_KV = 2
ROW_W = 2 * N_KV * HEAD_DIM
N_BRANCH = 3
WIN_A = 128
WIN_B = 512
CMP_LEN = 32
CMP_STRIDE = 16
SLC_BLK = 64
N_SEL = 16
PAGE_SIZE = 128
ROPE_THETA = 500000.0
ROT_DIM = HEAD_DIM // 4
EPS = 1e-6
FORCE_SCORE = 1e9
SCALE = HEAD_DIM ** -0.5
NEG = -1e30

LANES = 128
VMEM_LIMIT = 56 * 1024 * 1024

BF16 = jnp.bfloat16
F32 = jnp.float32


def _params(sem):
    return pltpu.CompilerParams(dimension_semantics=sem, vmem_limit_bytes=VMEM_LIMIT)


def _dot(a, b):
    return jnp.dot(a, b, preferred_element_type=F32)


def _dot_t(a, b):
    return lax.dot_general(a, b, (((1,), (1,)), ((), ())), preferred_element_type=F32)


def _rms(x, g):
    return x * lax.rsqrt(jnp.mean(x * x, axis=-1, keepdims=True) + EPS) * g


def _proj_kernel(x_ref, g_ref, w_ref, hg_ref, bd_ref, c_ref, s1_ref, s2_ref, *out_refs, plan):
    xn = _rms(x_ref[...], g_ref[...]).astype(BF16)
    y = _dot(xn, w_ref[...])
    cos, s1, s2 = c_ref[...], s1_ref[...], s2_ref[...]
    for blk, (kind, out_idx, out_blk, scale) in enumerate(plan):
        lanes = slice(blk * LANES, (blk + 1) * LANES)
        t = y[:, lanes]
        if kind == "rot":
            ms = _dot((t * t).astype(BF16), bd_ref[...])
            t = t * lax.rsqrt(ms + EPS) * hg_ref[:, lanes]
            t = t * cos + pltpu.roll(t, LANES - ROT_DIM // 2, 1) * s1 + pltpu.roll(t, ROT_DIM // 2, 1) * s2
        elif kind == "sig":
            t = jax.nn.sigmoid(t)
        if scale != 1.0:
            t = t * scale
        out_refs[out_idx][:, out_blk * LANES:(out_blk + 1) * LANES] = t


def _proj(x, g, w, hg, tabs, plan, out_widths, tm):
    m, d = x.shape
    n = w.shape[1]
    bd = jnp.kron(jnp.eye(LANES // HEAD_DIM, dtype=F32), jnp.full((HEAD_DIM, HEAD_DIM), 1.0 / HEAD_DIM, F32)).astype(BF16)
    row = lambda i: (i, 0)
    fix = lambda i: (0, 0)
    return pl.pallas_call(
        functools.partial(_proj_kernel, plan=tuple(plan)),
        grid=(m // tm,),
        in_specs=[pl.BlockSpec((tm, d), row), pl.BlockSpec((1, d), fix), pl.BlockSpec((d, n), fix),
                  pl.BlockSpec((1, n), fix), pl.BlockSpec((LANES, LANES), fix),
                  pl.BlockSpec((tm, LANES), row), pl.BlockSpec((tm, LANES), row), pl.BlockSpec((tm, LANES), row)],
        out_specs=[pl.BlockSpec((tm, wd), row) for wd in out_widths],
        out_shape=[jax.ShapeDtypeStruct((m, wd), F32) for wd in out_widths],
        compiler_params=_params(("parallel",)),
    )(x, g.reshape(1, d), w, hg.reshape(1, n), bd, *tabs)


def _mix_kernel(*refs, n_br, n_p, gated):
    pairs = [refs[2 * br:2 * br + 2] for br in range(n_br)]
    rest = refs[2 * n_br:]
    if gated:
        gt_ref, e_ref, w_ref, r_ref, out_ref = rest
        gt = gt_ref[...]
        g_hi = gt.astype(BF16)
        g_lo = (gt - g_hi.astype(F32)).astype(BF16)
    else:
        w_ref, r_ref, out_ref = rest
    in_prompt = pl.program_id(0) < n_p
    o = None
    for br, (p_ref, s_ref) in enumerate(pairs):
        term = jnp.where(in_prompt, p_ref[...], s_ref[...])
        if gated:
            term = term * (_dot(g_hi, e_ref[br]) + _dot(g_lo, e_ref[br]))
        o = term if o is None else o + term
    out_ref[...] = r_ref[...] + _dot(o.astype(BF16), w_ref[...])


def _mix(branches, w, res, tm, gates=None):
    m, d = res.shape
    mp, k = branches[0][0].shape
    n_p = mp // tm
    n_br = len(branches)
    row = lambda i: (i, 0)
    fix = lambda i: (0, 0)
    in_specs, args = [], []
    for o_p, o_s in branches:
        in_specs += [pl.BlockSpec((tm, k), lambda i: (jnp.minimum(i, n_p - 1), 0)),
                     pl.BlockSpec((tm, k), lambda i: (jnp.maximum(i - n_p, 0), 0))]
        args += [o_p, o_s]
    if gates is not None:
        n_heads = k // HEAD_DIM
        expand = (jnp.arange(LANES)[None, :, None] == (jnp.arange(n_br)[:, None, None] * n_heads
                                                       + jnp.arange(k)[None, None, :] // HEAD_DIM)).astype(BF16)
        in_specs += [pl.BlockSpec((tm, LANES), row), pl.BlockSpec((n_br, LANES, k), lambda i: (0, 0, 0))]
        args += [gates, expand]
    in_specs += [pl.BlockSpec((k, d), fix), pl.BlockSpec((tm, d), row)]
    args += [w, res]
    return pl.pallas_call(
        functools.partial(_mix_kernel, n_br=n_br, n_p=n_p, gated=gates is not None),
        grid=(m // tm,),
        in_specs=in_specs,
        out_specs=pl.BlockSpec((tm, d), row),
        out_shape=jax.ShapeDtypeStruct((m, d), F32),
        compiler_params=_params(("parallel",)),
    )(*args)


def _swiglu_kernel(h_ref, g_ref, wg_ref, wu_ref, w2_ref, out_ref, xn_ref, acc_ref):
    f = pl.program_id(1)

    @pl.when(f == 0)
    def _():
        xn_ref[...] = _rms(h_ref[...], g_ref[...]).astype(BF16)
        acc_ref[...] = h_ref[...]

    xn = xn_ref[...]
    gate = _dot(xn, wg_ref[...])
    up = _dot(xn, wu_ref[...])
    acc_ref[...] += _dot((jax.nn.silu(gate) * up).astype(BF16), w2_ref[...])

    @pl.when(f == pl.num_programs(1) - 1)
    def _():
        out_ref[...] = acc_ref[...]


def _swiglu(h, g, w13, w2, tm, tf):
    m, d = h.shape
    dff = w2.shape[0]
    nf = dff // tf
    return pl.pallas_call(
        _swiglu_kernel,
        grid=(m // tm, nf),
        in_specs=[pl.BlockSpec((tm, d), lambda i, f: (i, 0)), pl.BlockSpec((1, d), lambda i, f: (0, 0)),
                  pl.BlockSpec((d, tf), lambda i, f: (0, f)), pl.BlockSpec((d, tf), lambda i, f: (0, nf + f)),
                  pl.BlockSpec((tf, d), lambda i, f: (f, 0))],
        out_specs=pl.BlockSpec((tm, d), lambda i, f: (i, 0)),
        out_shape=jax.ShapeDtypeStruct((m, d), F32),
        scratch_shapes=[pltpu.VMEM((tm, d), BF16), pltpu.VMEM((tm, d), F32)],
        compiler_params=_params(("parallel", "arbitrary")),
    )(h, g.reshape(1, d), w13, w13, w2)


def _rot_tables(pos):
    half = ROT_DIM // 2
    inv = jnp.power(ROPE_THETA, -jnp.arange(half, dtype=F32) * (2.0 / ROT_DIM))
    ang = pos.astype(F32)[:, None] * inv[None, :]
    cos, sin = jnp.cos(ang), jnp.sin(ang)
    m = pos.shape[0]
    zeros = lambda n: jnp.zeros((m, n), F32)
    c = jnp.concatenate([cos, cos, jnp.ones((m, HEAD_DIM - ROT_DIM), F32)], axis=1)
    s1 = jnp.concatenate([-sin, zeros(HEAD_DIM - half)], axis=1)
    s2 = jnp.concatenate([zeros(half), sin, zeros(HEAD_DIM - ROT_DIM)], axis=1)
    reps = LANES // HEAD_DIM
    return tuple(jnp.tile(t, (1, reps)) for t in (c, s1, s2))


def _project_a(h, norm1, wqkv, qn, kn, tabs, tm):
    nq = wqkv.shape[1] - ROW_W
    hg = jnp.concatenate([jnp.tile(qn, nq // HEAD_DIM), jnp.tile(kn, N_KV), jnp.ones((N_KV * HEAD_DIM,), F32)])
    plan = [("rot", 0, c, SCALE) for c in range(nq // LANES)] + [("rot", 1, 0, 1.0), ("id", 1, 1, 1.0)]
    return _proj(h, norm1, wqkv.astype(BF16), hg, tabs, plan, (nq, ROW_W), tm)


def _project_kv(h, kv_norm, kv_w, kv_knorm, tabs, tm):
    hg = jnp.concatenate([jnp.concatenate([jnp.tile(kv_knorm[br], N_KV), jnp.ones((N_KV * HEAD_DIM,), F32)])
                          for br in range(N_BRANCH)])
    plan = []
    for br in range(N_BRANCH):
        plan += [("rot", br, 0, 1.0), ("id", br, 1, 1.0)]
    return _proj(h, kv_norm, kv_w.astype(BF16), hg, tabs, plan, (ROW_W,) * N_BRANCH, tm)


def _project_bq(h, norm1, wq, qn, tabs, tm):
    n_gate = wq.shape[1] % LANES
    nq = wq.shape[1] - n_gate
    wq = jnp.pad(wq, ((0, 0), (0, LANES - n_gate))).astype(BF16)
    hg = jnp.concatenate([jnp.tile(qn, nq // HEAD_DIM), jnp.ones((LANES,), F32)])
    plan = [("rot", 0, c, SCALE) for c in range(nq // LANES)] + [("sig", 1, 0, 1.0)]
    return _proj(h, norm1, wq, hg, tabs, plan, (nq, LANES), tm)


def _swap_halves(x):
    return pltpu.roll(x, HEAD_DIM, x.ndim - 1)


def _stack_heads(q, g):
    t, width = q.shape
    lane = lax.broadcasted_iota(jnp.int32, (t, LANES), 1)
    on_group = (lane >= HEAD_DIM) == (g == 1)
    parts = []
    for h in range(width // HEAD_DIM):
        src = q[:, (h // 2) * LANES:(h // 2 + 1) * LANES]
        own = (lane >= HEAD_DIM) == (h % 2 == 1)
        both = jnp.where(own, src, _swap_halves(src))
        parts.append(jnp.where(on_group, both, 0.0).astype(BF16))
    return jnp.concatenate(parts, axis=0)


def _unstack_heads(o, g):
    gp, t, _ = o.shape
    lane = lax.broadcasted_iota(jnp.int32, (t, LANES), 1)
    blocks = []
    for j in range(gp // 2):
        halves = []
        for h in (2 * j, 2 * j + 1):
            halves.append(jnp.where(g == h % 2, o[h], _swap_halves(o[h])))
        blocks.append(jnp.where(lane < HEAD_DIM, halves[0], halves[1]))
    return jnp.concatenate(blocks, axis=1)


def _flash_kernel(*refs, mode, window, tq, tk, gp, has_sink):
    if mode == "slc":
        q_ref, kv_ref, sel_ref, expand_ref, *rest = refs
    else:
        q_ref, kv_ref, *rest = refs
    if has_sink:
        sink_ref, *rest = rest
    o_ref, m_scr, acc_scr = rest
    g = pl.program_id(1)
    q0 = pl.program_id(2) * tq
    qz = _stack_heads(q_ref[...], g)
    m_scr[...] = jnp.full(m_scr.shape, NEG, F32)
    acc_scr[...] = jnp.zeros(acc_scr.shape, F32)
    qpos = q0 + lax.broadcasted_iota(jnp.int32, (tq, tk), 0)
    lo = jnp.maximum(q0 - (window - 1), 0) // tk if mode == "band" else 0
    hi = (q0 + tq - 1) // tk + 1
    reps = tk // LANES
    sum_lanes = (lax.broadcasted_iota(jnp.int32, (tk, LANES), 1) >= HEAD_DIM) != (g == 1)

    def body(kt, carry):
        k0 = pl.multiple_of(kt * tk, tk)
        kvt = kv_ref[pl.ds(k0, tk), :].astype(BF16)
        vt = jnp.where(sum_lanes, 1.0, kvt[:, LANES:])
        s_all = _dot_t(qz, kvt[:, :LANES])
        diff = qpos - (k0 + lax.broadcasted_iota(jnp.int32, (tq, tk), 1))
        if mode == "band":
            mask = (diff >= 0) & (diff < window)
        else:
            mask = (diff >= 0) & (_dot(sel_ref[...], expand_ref[kt]) > 0.5)
        ps = []
        for h in range(gp):
            s = jnp.where(mask, s_all[h * tq:(h + 1) * tq], NEG)
            m_prev = m_scr[h]
            m_new = jnp.maximum(m_prev, jnp.max(s, axis=-1, keepdims=True))
            alpha = jnp.exp(m_prev - m_new)
            ps.append(jnp.exp((s - jnp.concatenate([m_new] * reps, axis=1)).astype(BF16)))
            acc_scr[h] = alpha * acc_scr[h]
            m_scr[h] = m_new
        pv = _dot(jnp.concatenate(ps, axis=0), vt)
        for h in range(gp):
            acc_scr[h] += pv[h * tq:(h + 1) * tq]
        return carry

    lax.fori_loop(lo, hi, body, 0)
    m, acc = m_scr[...], acc_scr[...]
    l = _swap_halves(acc.reshape(gp * tq, LANES)).reshape(gp, tq, LANES)
    if has_sink:
        sink = jnp.concatenate([jnp.full((1, tq, LANES), sink_ref[g * gp + h], F32) for h in range(gp)], axis=0)
        m_fin = jnp.maximum(m, sink)
        scale = jnp.exp(m - m_fin)
        l = l * scale + jnp.exp(sink - m_fin)
        acc = acc * scale
    o_ref[...] = _unstack_heads(acc * (1.0 / l), g)


def _flash_prompt(q, rows, batch, mode, window, tq, tk, sel=None, sink=None):
    nq = q.shape[1]
    m = rows.shape[0]
    s_len = m // batch
    nqt = s_len // tq
    gw = nq // N_KV
    gp = gw // HEAD_DIM
    in_specs = [pl.BlockSpec((tq, gw), lambda b, g, i: (b * nqt + i, g)),
                pl.BlockSpec((None, s_len, ROW_W), lambda b, g, i: (b, 0, 0))]
    args = [q, rows.reshape(batch, s_len, ROW_W)]
    if mode == "slc":
        key_blk = (jnp.arange(s_len) // SLC_BLK).reshape(s_len // tk, 1, tk)
        expand = (key_blk == jnp.arange(LANES)[None, :, None]).astype(BF16)
        in_specs += [pl.BlockSpec((None, tq, LANES), lambda b, g, i: (g, b * nqt + i, 0)),
                     pl.BlockSpec(expand.shape, lambda b, g, i: (0, 0, 0))]
        args += [sel, expand]
    if sink is not None:
        in_specs.append(pl.BlockSpec(memory_space=pltpu.SMEM))
        args.append(sink)
    return pl.pallas_call(
        functools.partial(_flash_kernel, mode=mode, window=window, tq=tq, tk=tk, gp=gp, has_sink=sink is not None),
        grid=(batch, N_KV, nqt),
        in_specs=in_specs,
        out_specs=pl.BlockSpec((tq, gw), lambda b, g, i: (b * nqt + i, g)),
        out_shape=jax.ShapeDtypeStruct((m, nq), F32),
        scratch_shapes=[pltpu.VMEM((gp, tq, LANES), F32)] * 2,
        compiler_params=_params(("parallel", "parallel", "arbitrary")),
    )(*args)


def _top_blocks(score, n_sel):
    t, w = score.shape
    lane = lax.broadcasted_iota(jnp.int32, (t, w), 1)
    picked = jnp.zeros((t, w), F32)
    idx = jnp.zeros((t, LANES), jnp.int32)
    idx_lane = lax.broadcasted_iota(jnp.int32, (t, LANES), 1)
    for it in range(n_sel):
        top = jnp.max(score, axis=-1, keepdims=True)
        first = jnp.min(jnp.where(score == top, lane, w), axis=-1, keepdims=True)
        hit = lane == first
        picked = jnp.where(hit, 1.0, picked)
        score = jnp.where(hit, -jnp.inf, score)
        idx = jnp.where(idx_lane == it, first, idx)
    return picked, idx


def _pick_by_rank(score, n_sel, ns):
    lane = lax.broadcasted_iota(jnp.int32, score.shape, 1)
    beaten = jnp.zeros(score.shape, jnp.int32)
    for j in range(ns):
        col = score[:, j:j + 1]
        beaten = beaten + ((col > score) | ((col == score) & (lane > j))).astype(jnp.int32)
    return jnp.where((beaten < n_sel) & (lane < ns), 1.0, 0.0)


def _block_scores(psum, overlap_ref, qpos, ns):
    p_hi = psum.astype(BF16)
    p_lo = (psum - p_hi.astype(F32)).astype(BF16)
    imp = _dot(p_hi, overlap_ref[...]) + _dot(p_lo, overlap_ref[...])
    j = lax.broadcasted_iota(jnp.int32, imp.shape, 1)
    cur = qpos // SLC_BLK
    forced = (j == 0) | (j == cur) | (j == cur - 1)
    score = jnp.where(forced, FORCE_SCORE, jnp.where(j <= cur, imp, -1.0))
    return jnp.where(j < ns, score, -jnp.inf)


def _cmp_prompt_kernel(q_ref, kvc_ref, ov_ref, o_ref, sel_ref, *, tq, gp, nc, ns):
    g = pl.program_id(1)
    q0 = pl.program_id(2) * tq
    qz = _stack_heads(q_ref[...], g)
    kvc = kvc_ref[...].astype(BF16)
    ncp = kvc.shape[0]
    s_all = _dot_t(qz, kvc[:, :LANES])
    qpos = q0 + lax.broadcasted_iota(jnp.int32, (tq, ncp), 0)
    c = lax.broadcasted_iota(jnp.int32, (tq, ncp), 1)
    mask = (c * CMP_STRIDE + CMP_LEN - 1 <= qpos) & (c < nc)
    ps, psum = [], None
    for h in range(gp):
        p = _decode_softmax(s_all[h * tq:(h + 1) * tq], mask)
        psum = p if psum is None else psum + p
        ps.append(p.astype(BF16))
    o = _dot(jnp.concatenate(ps, axis=0), kvc[:, LANES:]).reshape(gp, tq, LANES)
    o_ref[...] = _unstack_heads(o, g)
    score = _block_scores(psum, ov_ref, q0 + lax.broadcasted_iota(jnp.int32, (tq, ov_ref.shape[1]), 0), ns)
    sel_ref[...] = _pick_by_rank(score, min(N_SEL, ns), ns).astype(BF16)


def _to_qz(q, batch):
    m, nq = q.shape
    t_len = m // batch
    gp = nq // (N_KV * HEAD_DIM)
    q5 = q.reshape(batch, t_len, N_KV, gp, HEAD_DIM).transpose(0, 2, 1, 3, 4)
    zero = jnp.zeros_like(q5[:, 0])
    qz = jnp.stack([jnp.concatenate([q5[:, 0], zero], axis=-1), jnp.concatenate([zero, q5[:, 1]], axis=-1)], axis=1)
    return qz.reshape(batch, N_KV, t_len * gp, LANES).astype(BF16)


def _from_qz(o, t_len):
    batch, _, rows, _ = o.shape
    gp = rows // t_len
    o6 = o.reshape(batch, N_KV, t_len, gp, LANES // HEAD_DIM, HEAD_DIM)
    x = jnp.stack([o6[:, g, :, :, g, :] for g in range(N_KV)], axis=2)
    return x.reshape(batch * t_len, N_KV * gp * HEAD_DIM)


def _decode_softmax(s, mask, sink=None):
    m = jnp.max(jnp.where(mask, s, NEG), axis=-1, keepdims=True)
    if sink is not None:
        m = jnp.maximum(m, sink)
    e = jnp.where(mask, jnp.exp(s - m), 0.0)
    den = jnp.sum(e, axis=-1, keepdims=True)
    if sink is not None:
        den = den + jnp.exp(sink - m)
    return e * (1.0 / jnp.where(den > 0, den, 1.0))


def _tiles(cache):
    nd = cache.ndim
    return jnp.transpose(cache, tuple(range(nd - 4)) + (nd - 3, nd - 2, nd - 1, nd - 4))


def _new_tiles(rows_new, batch):
    t_len = rows_new.shape[0] // batch
    t5 = _tiles(rows_new.reshape(batch, t_len, 2, N_KV, HEAD_DIM))
    return jnp.pad(t5, ((0, 0),) * 4 + ((0, LANES - t_len),))


def _to_qh(q, batch):
    m, nq = q.shape
    t_len = m // batch
    gp = nq // (N_KV * HEAD_DIM)
    return q.reshape(batch, t_len, N_KV, gp, HEAD_DIM).transpose(0, 2, 1, 3, 4).reshape(batch, N_KV, t_len * gp, HEAD_DIM)


def _from_qh(o, t_len):
    batch, _, rows, _ = o.shape
    gp = rows // t_len
    x = o.reshape(batch, N_KV, t_len, gp, HEAD_DIM).transpose(0, 2, 1, 3, 4)
    return x.reshape(batch * t_len, N_KV * gp * HEAD_DIM)


def _window_decode_kernel(*refs, bs, t_len, gp, window, has_sink):
    if has_sink:
        q_ref, buf_ref, new_ref, sink_ref, o_ref = refs
    else:
        q_ref, buf_ref, new_ref, o_ref = refs
    wb = buf_ref.shape[-1]
    nk = wb + LANES
    diff = (lax.broadcasted_iota(jnp.int32, (t_len, gp, nk), 0) + wb
            - lax.broadcasted_iota(jnp.int32, (t_len, gp, nk), 2))
    mask = (diff >= 0) & (diff < window)
    for b in range(bs):
        for g in range(N_KV):
            kt = jnp.concatenate([buf_ref[b, 0, g], new_ref[b, 0, g]], axis=1).astype(BF16)
            vt = jnp.concatenate([buf_ref[b, 1, g], new_ref[b, 1, g]], axis=1).astype(BF16)
            s = _dot(q_ref[b, g].astype(BF16), kt).reshape(t_len, gp, nk)
            sink = sink_ref[g].reshape(1, gp, 1) if has_sink else None
            p = _decode_softmax(s, mask, sink)
            o_ref[b, g] = _dot_t(p.reshape(t_len * gp, nk).astype(BF16), vt)


def _window_decode(qh, buf_t, new_t, window, t_len, bs, sink=None):
    batch, _, rows, _ = qh.shape
    wb = buf_t.shape[-1]
    gp = rows // t_len
    in_specs = [pl.BlockSpec((bs, N_KV, rows, HEAD_DIM), lambda i: (i, 0, 0, 0)),
                pl.BlockSpec((bs, 2, N_KV, HEAD_DIM, wb), lambda i: (i, 0, 0, 0, 0)),
                pl.BlockSpec((bs, 2, N_KV, HEAD_DIM, LANES), lambda i: (i, 0, 0, 0, 0))]
    args = [qh, buf_t, new_t]
    if sink is not None:
        in_specs.append(pl.BlockSpec((N_KV, gp, 1), lambda i: (0, 0, 0)))
        args.append(sink.reshape(N_KV, gp, 1))
    return pl.pallas_call(
        functools.partial(_window_decode_kernel, bs=bs, t_len=t_len, gp=gp, window=window, has_sink=sink is not None),
        grid=(batch // bs,),
        in_specs=in_specs,
        out_specs=pl.BlockSpec((bs, N_KV, rows, HEAD_DIM), lambda i: (i, 0, 0, 0)),
        out_shape=jax.ShapeDtypeStruct((batch, N_KV, rows, HEAD_DIM), F32),
        compiler_params=_params(("parallel",)),
    )(*args)


def _cmp_decode_kernel(qz_ref, kvc_ref, ov_ref, o_ref, idx_ref, *, bs, t_len, gp, nc, ns, past_len):
    ncp = kvc_ref.shape[1]
    qpos = past_len + lax.broadcasted_iota(jnp.int32, (t_len, gp, ncp), 0)
    c = lax.broadcasted_iota(jnp.int32, (t_len, gp, ncp), 2)
    mask = (c * CMP_STRIDE + CMP_LEN - 1 <= qpos) & (c < nc)
    qpos_blk = past_len + lax.broadcasted_iota(jnp.int32, (t_len, ov_ref.shape[1]), 0)
    scores = []
    for b in range(bs):
        kvc = kvc_ref[b].astype(BF16)
        for g in range(N_KV):
            s = _dot_t(qz_ref[b, g], kvc[:, :LANES]).reshape(t_len, gp, ncp)
            p = _decode_softmax(s, mask)
            o_ref[b, g] = _dot(p.reshape(t_len * gp, ncp).astype(BF16), kvc[:, LANES:])
            scores.append(_block_scores(jnp.sum(p, axis=1), ov_ref, qpos_blk, ns))
    _, idx = _top_blocks(jnp.concatenate(scores, axis=0), min(N_SEL, ns))
    for b in range(bs):
        for g in range(N_KV):
            i = b * N_KV + g
            idx_ref[b, g] = idx[i * t_len:(i + 1) * t_len]


def _cmp_decode(qz, kvc, nc, ns, past_len, t_len, bs):
    batch, _, rows, _ = qz.shape
    ncp = kvc.shape[1]
    nsp = -(-ns // LANES) * LANES
    return pl.pallas_call(
        functools.partial(_cmp_decode_kernel, bs=bs, t_len=t_len, gp=rows // t_len, nc=nc, ns=ns, past_len=past_len),
        grid=(batch // bs,),
        in_specs=[pl.BlockSpec((bs, N_KV, rows, LANES), lambda i: (i, 0, 0, 0)),
                  pl.BlockSpec((bs, ncp, ROW_W), lambda i: (i, 0, 0)),
                  pl.BlockSpec((ncp, nsp), lambda i: (0, 0))],
        out_specs=[pl.BlockSpec((bs, N_KV, rows, LANES), lambda i: (i, 0, 0, 0)),
                   pl.BlockSpec((bs, N_KV, t_len, LANES), lambda i: (i, 0, 0, 0))],
        out_shape=[jax.ShapeDtypeStruct((batch, N_KV, rows, LANES), F32),
                   jax.ShapeDtypeStruct((batch, N_KV, t_len, LANES), jnp.int32)],
        compiler_params=_params(("parallel",)),
    )(qz, kvc, _overlap(ncp, nsp))


SLC_RING = 3


def _slc_decode_kernel(idx_ref, tab_ref, q_ref, pool_ref, new_ref, o_ref, kbuf, sem, *, n_prob, t_len, gp, n_sel,
                       past_len):
    b = pl.program_id(0)
    pairs = n_prob // 2
    total_pairs = pl.num_programs(0) * pairs
    per_page = PAGE_SIZE // SLC_BLK
    past_blocks = past_len // SLC_BLK

    def block_copy(ref, page, g, sl, k):
        return pltpu.make_async_copy(ref.at[page, :, g], kbuf.at[sl, :, :, pl.ds(k * PAGE_SIZE, PAGE_SIZE)],
                                     sem.at[sl])

    def issue(bb, i, sl):
        g = i // t_len if isinstance(i, int) else lax.div(i, t_len)
        for k in range(n_sel):
            blk = idx_ref[bb, i * n_sel + k]
            page = tab_ref[bb, lax.div(jnp.minimum(blk, past_blocks - 1), per_page)]
            lax.cond(blk < past_blocks,
                     lambda: block_copy(pool_ref, page, g, sl, k).start(),
                     lambda: block_copy(new_ref, bb, g, sl, k).start())

    def drain(sl):
        pltpu.make_async_copy(kbuf.at[sl], kbuf.at[sl], sem.at[sl]).wait()

    def issue_pair(bb, jp, base):
        for u in range(2):
            issue(bb, 2 * jp + u, base + u)

    @pl.when(b == 0)
    def _():
        for pair in range(SLC_RING - 1):
            issue_pair(pair // pairs, pair % pairs, (pair % SLC_RING) * 2)

    nk = n_sel * PAGE_SIZE
    lane = lax.broadcasted_iota(jnp.int32, (gp, nk), 1)
    row = lane % PAGE_SIZE

    def attend(i, sl):
        t = i % t_len
        blk = jnp.zeros((gp, nk), jnp.int32)
        for k in range(n_sel):
            blk = jnp.where(lane // PAGE_SIZE == k, idx_ref[b, i * n_sel + k], blk)
        kpos = (blk // per_page) * PAGE_SIZE + row
        mask = (kpos // SLC_BLK == blk) & (kpos <= past_len + t)
        rows = pl.ds(pl.multiple_of(i * gp, gp), gp)
        s = _dot(q_ref[rows, :].astype(BF16), kbuf[sl, 0].astype(BF16))
        p = _decode_softmax(s, mask)
        o_ref[rows, :] = _dot_t(p.astype(BF16), kbuf[sl, 1].astype(BF16))

    def pair_step(jp, carry):
        pair = b * pairs + jp
        base = lax.rem(pair, SLC_RING) * 2
        for u in range(2):
            drain(base + u)

        @pl.when(pair + SLC_RING - 1 < total_pairs)
        def _():
            ahead = jp + SLC_RING - 1
            wrap = jnp.where(ahead >= pairs, 1, 0)
            issue_pair(b + wrap, ahead - wrap * pairs, lax.rem(pair + SLC_RING - 1, SLC_RING) * 2)

        for u in range(2):
            attend(2 * jp + u, base + u)
        return carry

    lax.fori_loop(0, pairs, pair_step, 0)


def _slc_decode(idx, page_table, qh, pool_tiles, new_tiles, past_len):
    batch, _, t_len, n_sel = idx.shape
    assert t_len <= PAGE_SIZE, "new rows must fit one page-sized tile"
    rows = qh.shape[2]
    n_prob = N_KV * t_len
    slots = 2 * SLC_RING
    grid_spec = pltpu.PrefetchScalarGridSpec(
        num_scalar_prefetch=2,
        grid=(batch,),
        in_specs=[pl.BlockSpec((None, N_KV * rows, HEAD_DIM), lambda b, *_: (b, 0, 0)),
                  pl.BlockSpec(memory_space=pl.ANY), pl.BlockSpec(memory_space=pl.ANY)],
        out_specs=pl.BlockSpec((None, N_KV * rows, HEAD_DIM), lambda b, *_: (b, 0, 0)),
        scratch_shapes=[pltpu.VMEM((slots, 2, HEAD_DIM, n_sel * PAGE_SIZE), F32), pltpu.SemaphoreType.DMA((slots,))],
    )
    out = pl.pallas_call(
        functools.partial(_slc_decode_kernel, n_prob=n_prob, t_len=t_len, gp=rows // t_len, n_sel=n_sel,
                          past_len=past_len),
        grid_spec=grid_spec,
        out_shape=jax.ShapeDtypeStruct((batch, N_KV * rows, HEAD_DIM), F32),
        compiler_params=_params(("arbitrary",)),
    )(idx.reshape(batch, n_prob * n_sel), page_table, qh.reshape(batch, N_KV * rows, HEAD_DIM), pool_tiles, new_tiles)
    return out.reshape(batch, N_KV, rows, HEAD_DIM)


def _compress_kernel(tab_ref, pool_ref, perm_ref, w1p_ref, w1e_ref, pe_ref, b1_ref, w2_ref, out_ref, xt, xl, sem,
                     *, pg, nj, npg):
    b, j = pl.program_id(0), pl.program_id(1)
    n = b * nj + j
    slot = n % 2
    per_page = PAGE_SIZE // CMP_STRIDE
    n_chunks = (pg + 1) * per_page
    c_out = pg * per_page

    def page_copy(bb, jj, sl, p):
        page = tab_ref[bb, jnp.minimum(jj * pg + p, npg - 1)]
        return pltpu.make_async_copy(pool_ref.at[page], xt.at[sl, p], sem.at[sl])

    def start(bb, jj, sl):
        lax.fori_loop(0, pg + 1, lambda p, c: (page_copy(bb, jj, sl, p).start(), c)[1], 0)

    def wait(bb, jj, sl):
        lax.fori_loop(0, pg + 1, lambda p, c: (page_copy(bb, jj, sl, p).wait(), c)[1], 0)

    @pl.when(n == 0)
    def _():
        start(b, j, slot)

    wait(b, j, slot)

    @pl.when(n + 1 < pl.num_programs(0) * nj)
    def _():
        nn = n + 1
        start(nn // nj, nn % nj, 1 - slot)

    def to_rows(p, c):
        t = _dot_t(perm_ref[...], xt[slot, p].astype(BF16))
        for l in range(CMP_STRIDE):
            xl[l, pl.ds(pl.multiple_of(p * per_page, per_page), per_page), :] = t[l * per_page:(l + 1) * per_page, :]
        return c

    lax.fori_loop(0, pg + 1, to_rows, 0, unroll=4)

    outs = []
    for s in range(2):
        acc = jnp.zeros((n_chunks, ROW_W), F32)
        for pair in range(CMP_STRIDE // 2):
            x0 = xl[2 * pair, :, s * LANES:(s + 1) * LANES]
            x1 = xl[2 * pair + 1, :, s * LANES:(s + 1) * LANES]
            acc = acc + _dot(jnp.concatenate([x0, x1], axis=1).astype(BF16), w1p_ref[s, pair])
        bias = _dot(pe_ref[s], w1e_ref[s])[0:1] + b1_ref[s]
        hidden = []
        for k in range(N_KV):
            a = acc[:, k * LANES:(k + 1) * LANES]
            nxt = pltpu.roll(_swap_halves(a), n_chunks - 1, 0)
            hidden.append(jax.nn.gelu(a + nxt + bias).astype(BF16))
        outs.append(_dot(jnp.concatenate(hidden, axis=1), w2_ref[s]))
    out_ref[...] = jnp.concatenate(outs, axis=1)[:c_out]


def _compress(pool, table, cmp_w1, cmp_b1, cmp_pe, cmp_w2, pg):
    nb, npg = table.shape
    per_page = PAGE_SIZE // CMP_STRIDE
    r = jnp.arange(PAGE_SIZE)
    perm = (r[None, :] == (r % per_page)[:, None] * CMP_STRIDE + (r // per_page)[:, None]).astype(BF16)
    nj = npg // pg
    hid = cmp_w1.shape[-1]
    w = cmp_w1.reshape(2, CMP_LEN // CMP_STRIDE, CMP_STRIDE, HEAD_DIM, hid)
    base = w.transpose(0, 2, 3, 1, 4).reshape(2, CMP_STRIDE, HEAD_DIM, 2 * hid)
    eye = jnp.eye(N_KV, dtype=F32)
    wsl = jnp.einsum("ab,sldn->sladbn", eye, base).reshape(2, CMP_STRIDE, LANES, ROW_W)
    w1p = wsl.reshape(2, CMP_STRIDE // 2, 2 * LANES, ROW_W).astype(BF16)
    w1e = jnp.pad(cmp_w1, ((0, 0), (0, 0), (0, LANES - hid))).astype(BF16)
    pe = jnp.broadcast_to(cmp_pe.reshape(2, 1, CMP_LEN * HEAD_DIM), (2, 8, CMP_LEN * HEAD_DIM)).astype(BF16)
    b1 = jnp.pad(cmp_b1, ((0, 0), (0, LANES - hid))).reshape(2, 1, LANES)
    w2h = jnp.pad(cmp_w2, ((0, 0), (0, LANES - hid), (0, 0)))
    w2c = jnp.einsum("ab,shd->sahbd", eye, w2h).reshape(2, ROW_W, N_KV * HEAD_DIM).astype(BF16)
    c_out = pg * PAGE_SIZE // CMP_STRIDE
    fix3 = lambda b, j, tab: (0, 0, 0)
    grid_spec = pltpu.PrefetchScalarGridSpec(
        num_scalar_prefetch=1,
        grid=(nb, nj),
        in_specs=[pl.BlockSpec(memory_space=pl.ANY), pl.BlockSpec(perm.shape, lambda b, j, tab: (0, 0)),
                  pl.BlockSpec(w1p.shape, lambda b, j, tab: (0, 0, 0, 0)), pl.BlockSpec(w1e.shape, fix3),
                  pl.BlockSpec(pe.shape, fix3), pl.BlockSpec(b1.shape, fix3), pl.BlockSpec(w2c.shape, fix3)],
        out_specs=pl.BlockSpec((None, c_out, ROW_W), lambda b, j, tab: (b, j, 0)),
        scratch_shapes=[pltpu.VMEM((2, pg + 1, ROW_W, PAGE_SIZE), F32),
                        pltpu.VMEM((CMP_STRIDE, (pg + 1) * per_page, ROW_W), F32),
                        pltpu.SemaphoreType.DMA((2,))],
    )
    return pl.pallas_call(
        functools.partial(_compress_kernel, pg=pg, nj=nj, npg=npg),
        grid_spec=grid_spec,
        out_shape=jax.ShapeDtypeStruct((nb, npg * per_page, ROW_W), F32),
        compiler_params=_params(("arbitrary", "arbitrary")),
    )(table, pool, perm, w1p, w1e, pe, b1, w2c)


def _overlap(ncp, nsp):
    ci = jnp.arange(ncp)[:, None] * CMP_STRIDE
    sj = jnp.arange(nsp)[None, :] * SLC_BLK
    return ((ci < sj + SLC_BLK) & (ci + CMP_LEN > sj)).astype(BF16)


def _cmp_prompt(q, kvc, batch, s_len, nc, tq):
    nq = q.shape[1]
    m = batch * s_len
    nqt = s_len // tq
    gw = nq // N_KV
    ncp = kvc.shape[1]
    ns = s_len // SLC_BLK
    return pl.pallas_call(
        functools.partial(_cmp_prompt_kernel, tq=tq, gp=gw // HEAD_DIM, nc=nc, ns=ns),
        grid=(batch, N_KV, nqt),
        in_specs=[pl.BlockSpec((tq, gw), lambda b, g, i: (b * nqt + i, g)),
                  pl.BlockSpec((None, ncp, ROW_W), lambda b, g, i: (b, 0, 0)),
                  pl.BlockSpec((ncp, LANES), lambda b, g, i: (0, 0))],
        out_specs=[pl.BlockSpec((tq, gw), lambda b, g, i: (b * nqt + i, g)),
                   pl.BlockSpec((None, tq, LANES), lambda b, g, i: (g, b * nqt + i, 0))],
        out_shape=[jax.ShapeDtypeStruct((m, nq), F32), jax.ShapeDtypeStruct((N_KV, m, LANES), BF16)],
        compiler_params=_params(("parallel", "parallel", "parallel")),
    )(q, kvc, _overlap(ncp, LANES))


TM = 512
TF = 1408
TQ = 128
DEC_BS = 8
CMP_PG_SAMPLE = 64


def kernel(x_prompt, x_sample, cache_swa_a, cache_cmp, cache_slc, cache_win_b, page_table, a_norm1, a_wqkv, a_qnorm, a_knorm, a_sink, a_wo, a_norm2, a_w13, a_w2, kv_norm, kv_w, kv_knorm, cmp_pe, cmp_w1, cmp_b1, cmp_w2, b_norm1, b_wq, b_qnorm, b_wo, b_norm2, b_w13, b_w2):
    b_p, s_len, d = x_prompt.shape
    b_s, t_len, _ = x_sample.shape
    mp, ms = b_p * s_len, b_s * t_len
    n_pool = cache_cmp.shape[0]
    past_len = page_table.shape[1] * PAGE_SIZE
    assert t_len < CMP_STRIDE, "new tokens must not complete a compression chunk"
    pos = jnp.concatenate([jnp.tile(jnp.arange(s_len, dtype=jnp.int32), b_p),
                           jnp.tile(past_len + jnp.arange(t_len, dtype=jnp.int32), b_s)])
    tabs = _rot_tables(pos)
    h = jnp.concatenate([x_prompt.reshape(mp, d), x_sample.reshape(ms, d)], axis=0)
    row5 = lambda r, b: r.reshape(b, -1, 2, N_KV, HEAD_DIM)

    swa_p, swa_s = [], []
    for l in range(a_norm1.shape[0]):
        q, rows = _project_a(h, a_norm1[l], a_wqkv[l], a_qnorm[l], a_knorm[l], tabs, TM)
        rows_p, rows_s = rows[:mp], rows[mp:]
        o_p = _flash_prompt(q, rows_p, b_p, "band", WIN_A, TQ, 128, sink=a_sink[l])
        wa = cache_swa_a.shape[2]
        o_h = _window_decode(_to_qh(q[mp:], b_s), _tiles(cache_swa_a[l]), _new_tiles(rows_s, b_s), WIN_A, t_len,
                             DEC_BS, sink=a_sink[l])
        h = _mix([(o_p, _from_qh(o_h, t_len))], a_wo[l].astype(BF16), h, TM)
        swa_p.append(row5(rows_p, b_p)[:, -min(WIN_A, s_len):])
        swa_s.append(jnp.concatenate([cache_swa_a[l], row5(rows_s, b_s)], axis=1)[:, -wa:])
        h = _swiglu(h, a_norm2[l], a_w13[l].astype(BF16), a_w2[l].astype(BF16), TM, TF)

    cmp_rows, slc_rows, win_rows = _project_kv(h, kv_norm, kv_w, kv_knorm, tabs, TM)
    cmp_p, slc_p, win_p = cmp_rows[:mp], slc_rows[:mp], win_rows[:mp]
    cmp_s, slc_s, win_s = cmp_rows[mp:], slc_rows[mp:], win_rows[mp:]
    pages_p = s_len // PAGE_SIZE
    kvc_p = _compress(cmp_p.reshape(b_p * pages_p, PAGE_SIZE, ROW_W).transpose(0, 2, 1),
                      jnp.arange(b_p * pages_p, dtype=jnp.int32).reshape(b_p, pages_p),
                      cmp_w1, cmp_b1, cmp_pe, cmp_w2, pages_p)
    kvc_s = _compress(_tiles(cache_cmp).reshape(n_pool, ROW_W, PAGE_SIZE), page_table, cmp_w1, cmp_b1, cmp_pe,
                      cmp_w2, CMP_PG_SAMPLE)
    nc_p = s_len // CMP_STRIDE - CMP_LEN // CMP_STRIDE + 1
    nc_s = past_len // CMP_STRIDE - CMP_LEN // CMP_STRIDE + 1
    ns_s = -(-(past_len + t_len) // SLC_BLK)
    wb = cache_win_b.shape[1]

    for j in range(b_norm1.shape[0]):
        q, gates = _project_bq(h, b_norm1[j], b_wq[j], b_qnorm[j], tabs, TM)
        oc_p, sel = _cmp_prompt(q, kvc_p, b_p, s_len, nc_p, TQ)
        os_p = _flash_prompt(q, slc_p, b_p, "slc", 0, TQ, 256, sel=sel)
        ow_p = _flash_prompt(q, win_p, b_p, "band", WIN_B, TQ, 256)
        qh = _to_qh(q[mp:], b_s)
        oc_z, idx = _cmp_decode(_to_qz(q[mp:], b_s), kvc_s, nc_s, ns_s, past_len, t_len, DEC_BS // 2)
        os_h = _slc_decode(idx[..., :min(N_SEL, ns_s)], page_table, qh, _tiles(cache_slc), _new_tiles(slc_s, b_s),
                           past_len)
        ow_h = _window_decode(qh, _tiles(cache_win_b), _new_tiles(win_s, b_s), WIN_B, t_len, DEC_BS)
        h = _mix([(oc_p, _from_qz(oc_z, t_len)), (os_p, _from_qh(os_h, t_len)), (ow_p, _from_qh(ow_h, t_len))],
                 b_wo[j].astype(BF16), h, TM, gates=gates)
        h = _swiglu(h, b_norm2[j], b_w13[j].astype(BF16), b_w2[j].astype(BF16), TM, TF)

    hp = h[:mp].reshape(b_p, s_len, d)
    hs = h[mp:].reshape(b_s, t_len, d)
    win_b_prompt = row5(win_p, b_p)[:, -min(WIN_B, s_len):]
    win_b_sample = jnp.concatenate([cache_win_b, row5(win_s, b_s)], axis=1)[:, -wb:]
    return (hp, hs, jnp.stack(swa_p, axis=0), jnp.stack(swa_s, axis=0),
            row5(cmp_p, b_p), row5(cmp_s, b_s), row5(slc_p, b_p), row5(slc_s, b_s), win_b_prompt, win_b_sample)
```

```python
import functools

import jax
import jax.numpy as jnp
from jax import lax
from jax.experimental import pallas as pl
from jax.experimental.pallas import tpu as pltpu

HEAD_DIM = 64
N_KV = 2
ROW_W = 2 * N_KV * HEAD_DIM
N_BRANCH = 3
WIN_A = 128
WIN_B = 512
CMP_LEN = 32
CMP_STRIDE = 16
SLC_BLK = 64
N_SEL = 16
PAGE_SIZE = 128
ROPE_THETA = 500000.0
ROT_DIM = HEAD_DIM // 4
EPS = 1e-6
FORCE_SCORE = 1e9
SCALE = HEAD_DIM ** -0.5
NEG = -1e30

LANES = 128
VMEM_LIMIT = 56 * 1024 * 1024

BF16 = jnp.bfloat16
F32 = jnp.float32


def _params(sem):
    return pltpu.CompilerParams(dimension_semantics=sem, vmem_limit_bytes=VMEM_LIMIT)


def _dot(a, b):
    return jnp.dot(a, b, preferred_element_type=F32)


def _dot_t(a, b):
    return lax.dot_general(a, b, (((1,), (1,)), ((), ())), preferred_element_type=F32)


def _rms(x, g):
    return x * lax.rsqrt(jnp.mean(x * x, axis=-1, keepdims=True) + EPS) * g


def _proj_kernel(x_ref, g_ref, w_ref, hg_ref, bd_ref, c_ref, s1_ref, s2_ref, *out_refs, plan, n_tiled_steps):
    xn = _rms(x_ref[...], g_ref[...]).astype(BF16)
    y = _dot(xn, w_ref[...])
    cos, s1, s2 = c_ref[...], s1_ref[...], s2_ref[...]
    for blk, (kind, out_idx, out_blk, scale, tile_idx) in enumerate(plan):
        lanes = slice(blk * LANES, (blk + 1) * LANES)
        t = y[:, lanes]
        if kind == "rot":
            ms = _dot((t * t).astype(BF16), bd_ref[...])
            t = t * lax.rsqrt(ms + EPS) * hg_ref[:, lanes]
            t = t * cos + pltpu.roll(t, LANES - ROT_DIM // 2, 1) * s1 + pltpu.roll(t, ROT_DIM // 2, 1) * s2
        elif kind == "sig":
            t = jax.nn.sigmoid(t)
        if scale != 1.0:
            t = t * scale
        out_refs[out_idx][:, out_blk * LANES:(out_blk + 1) * LANES] = t
        if tile_idx is not None:
            @pl.when(pl.program_id(0) < n_tiled_steps)
            def _():
                out_refs[tile_idx][out_blk * LANES:(out_blk + 1) * LANES, :] = t.T


def _proj(x, g, w, hg, tabs, plan, out_widths, tm, tiles=None):
    m, d = x.shape
    n = w.shape[1]
    bd = jnp.kron(jnp.eye(LANES // HEAD_DIM, dtype=F32), jnp.full((HEAD_DIM, HEAD_DIM), 1.0 / HEAD_DIM, F32)).astype(BF16)
    row = lambda i: (i, 0)
    fix = lambda i: (0, 0)
    out_specs = [pl.BlockSpec((tm, wd), row) for wd in out_widths]
    out_shape = [jax.ShapeDtypeStruct((m, wd), F32) for wd in out_widths]
    n_p = 0
    if tiles is not None:
        n_tiled, batch, s_len = tiles
        per_seq = s_len // tm
        n_p = batch * per_seq
        tile_map = lambda i: (jnp.minimum(i // per_seq, batch - 1), 0, jnp.where(i < n_p, i % per_seq, per_seq - 1))
        out_specs += [pl.BlockSpec((None, ROW_W, tm), tile_map)] * n_tiled
        out_shape += [jax.ShapeDtypeStruct((batch, ROW_W, s_len), F32)] * n_tiled
    return pl.pallas_call(
        functools.partial(_proj_kernel, plan=tuple(plan), n_tiled_steps=n_p),
        grid=(m // tm,),
        in_specs=[pl.BlockSpec((tm, d), row), pl.BlockSpec((1, d), fix), pl.BlockSpec((d, n), fix),
                  pl.BlockSpec((1, n), fix), pl.BlockSpec((LANES, LANES), fix),
                  pl.BlockSpec((tm, LANES), row), pl.BlockSpec((tm, LANES), row), pl.BlockSpec((tm, LANES), row)],
        out_specs=out_specs,
        out_shape=out_shape,
        compiler_params=_params(("arbitrary",)),
    )(x, g.reshape(1, d), w, hg.reshape(1, n), bd, *tabs)


def _mix_kernel(*refs, n_br, n_p, gated):
    pairs = [refs[2 * br:2 * br + 2] for br in range(n_br)]
    rest = refs[2 * n_br:]
    if gated:
        gt_ref, e_ref, w_ref, r_ref, out_ref = rest
        gt = gt_ref[...]
        g_hi = gt.astype(BF16)
        g_lo = (gt - g_hi.astype(F32)).astype(BF16)
    else:
        w_ref, r_ref, out_ref = rest
    in_prompt = pl.program_id(0) < n_p
    o = None
    for br, (p_ref, s_ref) in enumerate(pairs):
        term = jnp.where(in_prompt, p_ref[...], s_ref[...])
        if gated:
            term = term * (_dot(g_hi, e_ref[br]) + _dot(g_lo, e_ref[br]))
        o = term if o is None else o + term
    out_ref[...] = r_ref[...] + _dot(o.astype(BF16), w_ref[...])


def _mix(branches, w, res, tm, gates=None):
    m, d = res.shape
    mp, k = branches[0][0].shape
    n_p = mp // tm
    n_br = len(branches)
    row = lambda i: (i, 0)
    fix = lambda i: (0, 0)
    in_specs, args = [], []
    for o_p, o_s in branches:
        in_specs += [pl.BlockSpec((tm, k), lambda i: (jnp.minimum(i, n_p - 1), 0)),
                     pl.BlockSpec((tm, k), lambda i: (jnp.maximum(i - n_p, 0), 0))]
        args += [o_p, o_s]
    if gates is not None:
        n_heads = k // HEAD_DIM
        expand = (jnp.arange(LANES)[None, :, None] == (jnp.arange(n_br)[:, None, None] * n_heads
                                                       + jnp.arange(k)[None, None, :] // HEAD_DIM)).astype(BF16)
        in_specs += [pl.BlockSpec((tm, LANES), row), pl.BlockSpec((n_br, LANES, k), lambda i: (0, 0, 0))]
        args += [gates, expand]
    in_specs += [pl.BlockSpec((k, d), fix), pl.BlockSpec((tm, d), row)]
    args += [w, res]
    return pl.pallas_call(
        functools.partial(_mix_kernel, n_br=n_br, n_p=n_p, gated=gates is not None),
        grid=(m // tm,),
        in_specs=in_specs,
        out_specs=pl.BlockSpec((tm, d), row),
        out_shape=jax.ShapeDtypeStruct((m, d), F32),
        compiler_params=_params(("parallel",)),
    )(*args)


def _swiglu_kernel(h_ref, g_ref, wg_ref, wu_ref, w2_ref, out_ref, xn_ref, acc_ref):
    f = pl.program_id(1)

    @pl.when(f == 0)
    def _():
        xn_ref[...] = _rms(h_ref[...], g_ref[...]).astype(BF16)
        acc_ref[...] = h_ref[...]

    xn = xn_ref[...]
    gate = _dot(xn, wg_ref[...])
    up = _dot(xn, wu_ref[...])
    acc_ref[...] += _dot((jax.nn.silu(gate) * up).astype(BF16), w2_ref[...])

    @pl.when(f == pl.num_programs(1) - 1)
    def _():
        out_ref[...] = acc_ref[...]


def _swiglu(h, g, w13, w2, tm, tf):
    m, d = h.shape
    dff = w2.shape[0]
    nf = dff // tf
    return pl.pallas_call(
        _swiglu_kernel,
        grid=(m // tm, nf),
        in_specs=[pl.BlockSpec((tm, d), lambda i, f: (i, 0)), pl.BlockSpec((1, d), lambda i, f: (0, 0)),
                  pl.BlockSpec((d, tf), lambda i, f: (0, f)), pl.BlockSpec((d, tf), lambda i, f: (0, nf + f)),
                  pl.BlockSpec((tf, d), lambda i, f: (f, 0))],
        out_specs=pl.BlockSpec((tm, d), lambda i, f: (i, 0)),
        out_shape=jax.ShapeDtypeStruct((m, d), F32),
        scratch_shapes=[pltpu.VMEM((tm, d), BF16), pltpu.VMEM((tm, d), F32)],
        compiler_params=_params(("parallel", "arbitrary")),
    )(h, g.reshape(1, d), w13, w13, w2)


def _rot_tables(pos):
    half = ROT_DIM // 2
    inv = jnp.power(ROPE_THETA, -jnp.arange(half, dtype=F32) * (2.0 / ROT_DIM))
    ang = pos.astype(F32)[:, None] * inv[None, :]
    cos, sin = jnp.cos(ang), jnp.sin(ang)
    m = pos.shape[0]
    zeros = lambda n: jnp.zeros((m, n), F32)
    c = jnp.concatenate([cos, cos, jnp.ones((m, HEAD_DIM - ROT_DIM), F32)], axis=1)
    s1 = jnp.concatenate([-sin, zeros(HEAD_DIM - half)], axis=1)
    s2 = jnp.concatenate([zeros(half), sin, zeros(HEAD_DIM - ROT_DIM)], axis=1)
    reps = LANES // HEAD_DIM
    return tuple(jnp.tile(t, (1, reps)) for t in (c, s1, s2))


def _project_a(h, norm1, wqkv, qn, kn, tabs, tm):
    nq = wqkv.shape[1] - ROW_W
    hg = jnp.concatenate([jnp.tile(qn, nq // HEAD_DIM), jnp.tile(kn, N_KV), jnp.ones((N_KV * HEAD_DIM,), F32)])
    plan = ([("rot", 0, c, SCALE, None) for c in range(nq // LANES)]
            + [("rot", 1, 0, 1.0, None), ("id", 1, 1, 1.0, None)])
    return _proj(h, norm1, wqkv.astype(BF16), hg, tabs, plan, (nq, ROW_W), tm)


def _project_kv(h, kv_norm, kv_w, kv_knorm, tabs, tm, batch, s_len):
    hg = jnp.concatenate([jnp.concatenate([jnp.tile(kv_knorm[br], N_KV), jnp.ones((N_KV * HEAD_DIM,), F32)])
                          for br in range(N_BRANCH)])
    plan = []
    for br in range(N_BRANCH):
        plan += [("rot", br, 0, 1.0, N_BRANCH + br), ("id", br, 1, 1.0, N_BRANCH + br)]
    return _proj(h, kv_norm, kv_w.astype(BF16), hg, tabs, plan, (ROW_W,) * N_BRANCH, tm,
                 tiles=(N_BRANCH, batch, s_len))


def _project_bq(h, norm1, wq, qn, tabs, tm):
    n_gate = wq.shape[1] % LANES
    nq = wq.shape[1] - n_gate
    wq = jnp.pad(wq, ((0, 0), (0, LANES - n_gate))).astype(BF16)
    hg = jnp.concatenate([jnp.tile(qn, nq // HEAD_DIM), jnp.ones((LANES,), F32)])
    plan = [("rot", 0, c, SCALE, None) for c in range(nq // LANES)] + [("sig", 1, 0, 1.0, None)]
    return _proj(h, norm1, wq, hg, tabs, plan, (nq, LANES), tm)


def _swap_halves(x):
    return pltpu.roll(x, HEAD_DIM, x.ndim - 1)


def _stack_heads(q, g):
    t, width = q.shape
    lane = lax.broadcasted_iota(jnp.int32, (t, LANES), 1)
    on_group = (lane >= HEAD_DIM) == (g == 1)
    parts = []
    for h in range(width // HEAD_DIM):
        src = q[:, (h // 2) * LANES:(h // 2 + 1) * LANES]
        own = (lane >= HEAD_DIM) == (h % 2 == 1)
        both = jnp.where(own, src, _swap_halves(src))
        parts.append(jnp.where(on_group, both, 0.0).astype(BF16))
    return jnp.concatenate(parts, axis=0)


def _unstack_heads(o, g):
    gp, t, _ = o.shape
    lane = lax.broadcasted_iota(jnp.int32, (t, LANES), 1)
    blocks = []
    for j in range(gp // 2):
        halves = []
        for h in (2 * j, 2 * j + 1):
            halves.append(jnp.where(g == h % 2, o[h], _swap_halves(o[h])))
        blocks.append(jnp.where(lane < HEAD_DIM, halves[0], halves[1]))
    return jnp.concatenate(blocks, axis=1)


def _flash_kernel(*refs, mode, window, tq, tk, gp, has_sink):
    if mode == "slc":
        q_ref, kv_ref, sel_ref, expand_ref, *rest = refs
    else:
        q_ref, kv_ref, *rest = refs
    if has_sink:
        sink_ref, *rest = rest
    o_ref, m_scr, acc_scr = rest
    g = pl.program_id(1)
    q0 = pl.program_id(2) * tq
    qz = _stack_heads(q_ref[...], g)
    m_scr[...] = jnp.full(m_scr.shape, NEG, F32)
    acc_scr[...] = jnp.zeros(acc_scr.shape, F32)
    qpos = q0 + lax.broadcasted_iota(jnp.int32, (tq, tk), 0)
    lo = jnp.maximum(q0 - (window - 1), 0) // tk if mode == "band" else 0
    hi = (q0 + tq - 1) // tk + 1
    reps = tk // LANES
    sum_lanes = (lax.broadcasted_iota(jnp.int32, (tk, LANES), 1) >= HEAD_DIM) != (g == 1)

    def body(kt, carry):
        k0 = pl.multiple_of(kt * tk, tk)
        kvt = kv_ref[pl.ds(k0, tk), :].astype(BF16)
        vt = jnp.where(sum_lanes, 1.0, kvt[:, LANES:])
        s_all = _dot_t(qz, kvt[:, :LANES])
        diff = qpos - (k0 + lax.broadcasted_iota(jnp.int32, (tq, tk), 1))
        if mode == "band":
            mask = (diff >= 0) & (diff < window)
        else:
            mask = (diff >= 0) & (_dot(sel_ref[...], expand_ref[kt]) > 0.5)
        ps = []
        for h in range(gp):
            s = jnp.where(mask, s_all[h * tq:(h + 1) * tq], NEG)
            m_prev = m_scr[h]
            m_new = jnp.maximum(m_prev, jnp.max(s, axis=-1, keepdims=True))
            alpha = jnp.exp(m_prev - m_new)
            ps.append(jnp.exp((s - jnp.concatenate([m_new] * reps, axis=1)).astype(BF16)))
            acc_scr[h] = alpha * acc_scr[h]
            m_scr[h] = m_new
        pv = _dot(jnp.concatenate(ps, axis=0), vt)
        for h in range(gp):
            acc_scr[h] += pv[h * tq:(h + 1) * tq]
        return carry

    lax.fori_loop(lo, hi, body, 0)
    m, acc = m_scr[...], acc_scr[...]
    l = _swap_halves(acc.reshape(gp * tq, LANES)).reshape(gp, tq, LANES)
    if has_sink:
        sink = jnp.concatenate([jnp.full((1, tq, LANES), sink_ref[g * gp + h], F32) for h in range(gp)], axis=0)
        m_fin = jnp.maximum(m, sink)
        scale = jnp.exp(m - m_fin)
        l = l * scale + jnp.exp(sink - m_fin)
        acc = acc * scale
    o_ref[...] = _unstack_heads(acc * (1.0 / l), g)


def _flash_prompt(q, rows, batch, s_len, mode, window, tq, tk, sel=None, sink=None):
    nq = q.shape[1]
    m = batch * s_len
    nqt = s_len // tq
    gw = nq // N_KV
    gp = gw // HEAD_DIM
    in_specs = [pl.BlockSpec((tq, gw), lambda b, g, i: (b * nqt + i, g)),
                pl.BlockSpec((s_len, ROW_W), lambda b, g, i: (b, 0))]
    args = [q, rows]
    if mode == "slc":
        key_blk = (jnp.arange(s_len) // SLC_BLK).reshape(s_len // tk, 1, tk)
        expand = (key_blk == jnp.arange(LANES)[None, :, None]).astype(BF16)
        in_specs += [pl.BlockSpec((None, tq, LANES), lambda b, g, i: (g, b * nqt + i, 0)),
                     pl.BlockSpec(expand.shape, lambda b, g, i: (0, 0, 0))]
        args += [sel, expand]
    if sink is not None:
        in_specs.append(pl.BlockSpec(memory_space=pltpu.SMEM))
        args.append(sink)
    return pl.pallas_call(
        functools.partial(_flash_kernel, mode=mode, window=window, tq=tq, tk=tk, gp=gp, has_sink=sink is not None),
        grid=(batch, N_KV, nqt),
        in_specs=in_specs,
        out_specs=pl.BlockSpec((tq, gw), lambda b, g, i: (b * nqt + i, g)),
        out_shape=jax.ShapeDtypeStruct((m, nq), F32),
        scratch_shapes=[pltpu.VMEM((gp, tq, LANES), F32)] * 2,
        compiler_params=_params(("parallel", "parallel", "arbitrary")),
    )(*args)


def _top_blocks(score, n_sel):
    t, w = score.shape
    lane = lax.broadcasted_iota(jnp.int32, (t, w), 1)
    picked = jnp.zeros((t, w), F32)
    idx = jnp.zeros((t, LANES), jnp.int32)
    idx_lane = lax.broadcasted_iota(jnp.int32, (t, LANES), 1)
    for it in range(n_sel):
        top = jnp.max(score, axis=-1, keepdims=True)
        first = jnp.min(jnp.where(score == top, lane, w), axis=-1, keepdims=True)
        hit = lane == first
        picked = jnp.where(hit, 1.0, picked)
        score = jnp.where(hit, -jnp.inf, score)
        idx = jnp.where(idx_lane == it, first, idx)
    return picked, idx


def _pick_by_rank(score_t, n_sel, ns):
    blk = lax.broadcasted_iota(jnp.int32, score_t.shape, 0)
    beaten = jnp.zeros(score_t.shape, jnp.int32)
    for j in range(ns):
        row = jnp.broadcast_to(score_t[j:j + 1, :], score_t.shape)
        beaten = beaten + jnp.where((row > score_t) | ((row == score_t) & (blk > j)), 1, 0)
    return jnp.where((beaten < n_sel) & (blk < ns), 1.0, 0.0)


def _block_scores_t(psum, overlap_t_ref, qpos_t, ns):
    p_hi = psum.astype(BF16)
    p_lo = (psum - p_hi.astype(F32)).astype(BF16)
    imp = _dot_t(overlap_t_ref[...], p_hi) + _dot_t(overlap_t_ref[...], p_lo)
    j = lax.broadcasted_iota(jnp.int32, imp.shape, 0)
    cur = qpos_t // SLC_BLK
    forced = (j == 0) | (j == cur) | (j == cur - 1)
    score = jnp.where(forced, FORCE_SCORE, jnp.where(j <= cur, imp, -1.0))
    return jnp.where(j < ns, score, -jnp.inf)


def _block_scores(psum, overlap_ref, qpos, ns):
    p_hi = psum.astype(BF16)
    p_lo = (psum - p_hi.astype(F32)).astype(BF16)
    imp = _dot(p_hi, overlap_ref[...]) + _dot(p_lo, overlap_ref[...])
    j = lax.broadcasted_iota(jnp.int32, imp.shape, 1)
    cur = qpos // SLC_BLK
    forced = (j == 0) | (j == cur) | (j == cur - 1)
    score = jnp.where(forced, FORCE_SCORE, jnp.where(j <= cur, imp, -1.0))
    return jnp.where(j < ns, score, -jnp.inf)


def _cmp_prompt_kernel(q_ref, kvc_ref, ov_ref, o_ref, sel_ref, *, tq, gp, nc, ns):
    g = pl.program_id(1)
    q0 = pl.program_id(2) * tq
    qz = _stack_heads(q_ref[...], g)
    kvc = kvc_ref[...].astype(BF16)
    ncp = kvc.shape[0]
    s_all = _dot_t(qz, kvc[:, :LANES])
    qpos = q0 + lax.broadcasted_iota(jnp.int32, (tq, ncp), 0)
    c = lax.broadcasted_iota(jnp.int32, (tq, ncp), 1)
    mask = (c * CMP_STRIDE + CMP_LEN - 1 <= qpos) & (c < nc)
    ps, psum = [], None
    for h in range(gp):
        p = _decode_softmax(s_all[h * tq:(h + 1) * tq], mask)
        psum = p if psum is None else psum + p
        ps.append(p.astype(BF16))
    o = _dot(jnp.concatenate(ps, axis=0), kvc[:, LANES:]).reshape(gp, tq, LANES)
    o_ref[...] = _unstack_heads(o, g)
    score_t = _block_scores_t(psum, ov_ref, q0 + lax.broadcasted_iota(jnp.int32, (LANES, tq), 1), ns)
    ns_rows = -(-ns // 8) * 8
    picks_t = _pick_by_rank(score_t[:ns_rows], min(N_SEL, ns), ns)
    picks_t = jnp.concatenate([picks_t, jnp.zeros((LANES - ns_rows, tq), F32)], axis=0)
    sel_ref[...] = picks_t.T.astype(BF16)


def _to_qz(q, batch):
    m, nq = q.shape
    t_len = m // batch
    gp = nq // (N_KV * HEAD_DIM)
    q5 = q.reshape(batch, t_len, N_KV, gp, HEAD_DIM).transpose(0, 2, 1, 3, 4)
    zero = jnp.zeros_like(q5[:, 0])
    qz = jnp.stack([jnp.concatenate([q5[:, 0], zero], axis=-1), jnp.concatenate([zero, q5[:, 1]], axis=-1)], axis=1)
    return qz.reshape(batch, N_KV, t_len * gp, LANES).astype(BF16)


def _from_qz(o, t_len):
    batch, _, rows, _ = o.shape
    gp = rows // t_len
    o6 = o.reshape(batch, N_KV, t_len, gp, LANES // HEAD_DIM, HEAD_DIM)
    x = jnp.stack([o6[:, g, :, :, g, :] for g in range(N_KV)], axis=2)
    return x.reshape(batch * t_len, N_KV * gp * HEAD_DIM)


def _decode_softmax(s, mask, sink=None):
    m = jnp.max(jnp.where(mask, s, NEG), axis=-1, keepdims=True)
    if sink is not None:
        m = jnp.maximum(m, sink)
    e = jnp.where(mask, jnp.exp(s - m), 0.0)
    den = jnp.sum(e, axis=-1, keepdims=True)
    if sink is not None:
        den = den + jnp.exp(sink - m)
    return e * (1.0 / jnp.where(den > 0, den, 1.0))


def _tiles(cache):
    nd = cache.ndim
    return jnp.transpose(cache, tuple(range(nd - 4)) + (nd - 3, nd - 2, nd - 1, nd - 4))


def _new_tiles(rows_new, batch):
    t_len = rows_new.shape[0] // batch
    t5 = _tiles(rows_new.reshape(batch, t_len, 2, N_KV, HEAD_DIM))
    return jnp.pad(t5, ((0, 0),) * 4 + ((0, LANES - t_len),))


def _to_qh(q, batch):
    m, nq = q.shape
    t_len = m // batch
    gp = nq // (N_KV * HEAD_DIM)
    return q.reshape(batch, t_len, N_KV, gp, HEAD_DIM).transpose(0, 2, 1, 3, 4).reshape(batch, N_KV, t_len * gp, HEAD_DIM)


def _from_qh(o, t_len):
    batch, _, rows, _ = o.shape
    gp = rows // t_len
    x = o.reshape(batch, N_KV, t_len, gp, HEAD_DIM).transpose(0, 2, 1, 3, 4)
    return x.reshape(batch * t_len, N_KV * gp * HEAD_DIM)


def _window_decode_kernel(*refs, bs, t_len, gp, window, has_sink):
    if has_sink:
        q_ref, buf_ref, new_ref, sink_ref, o_ref = refs
    else:
        q_ref, buf_ref, new_ref, o_ref = refs
    wb = buf_ref.shape[-1]
    nk = wb + LANES
    diff = (lax.broadcasted_iota(jnp.int32, (t_len, gp, nk), 0) + wb
            - lax.broadcasted_iota(jnp.int32, (t_len, gp, nk), 2))
    mask = (diff >= 0) & (diff < window)
    for b in range(bs):
        for g in range(N_KV):
            kt = jnp.concatenate([buf_ref[b, 0, g], new_ref[b, 0, g]], axis=1).astype(BF16)
            vt = jnp.concatenate([buf_ref[b, 1, g], new_ref[b, 1, g]], axis=1).astype(BF16)
            s = _dot(q_ref[b, g].astype(BF16), kt).reshape(t_len, gp, nk)
            sink = sink_ref[g].reshape(1, gp, 1) if has_sink else None
            p = _decode_softmax(s, mask, sink)
            o_ref[b, g] = _dot_t(p.reshape(t_len * gp, nk).astype(BF16), vt)


def _window_decode(qh, buf_t, new_t, window, t_len, bs, sink=None):
    batch, _, rows, _ = qh.shape
    wb = buf_t.shape[-1]
    gp = rows // t_len
    in_specs = [pl.BlockSpec((bs, N_KV, rows, HEAD_DIM), lambda i: (i, 0, 0, 0)),
                pl.BlockSpec((bs, 2, N_KV, HEAD_DIM, wb), lambda i: (i, 0, 0, 0, 0)),
                pl.BlockSpec((bs, 2, N_KV, HEAD_DIM, LANES), lambda i: (i, 0, 0, 0, 0))]
    args = [qh, buf_t, new_t]
    if sink is not None:
        in_specs.append(pl.BlockSpec((N_KV, gp, 1), lambda i: (0, 0, 0)))
        args.append(sink.reshape(N_KV, gp, 1))
    return pl.pallas_call(
        functools.partial(_window_decode_kernel, bs=bs, t_len=t_len, gp=gp, window=window, has_sink=sink is not None),
        grid=(batch // bs,),
        in_specs=in_specs,
        out_specs=pl.BlockSpec((bs, N_KV, rows, HEAD_DIM), lambda i: (i, 0, 0, 0)),
        out_shape=jax.ShapeDtypeStruct((batch, N_KV, rows, HEAD_DIM), F32),
        compiler_params=_params(("parallel",)),
    )(*args)


def _cmp_decode_kernel(qz_ref, kvc_ref, ov_ref, o_ref, idx_ref, *, bs, t_len, gp, nc, ns, past_len):
    ncp = kvc_ref.shape[1]
    qpos = past_len + lax.broadcasted_iota(jnp.int32, (t_len, gp, ncp), 0)
    c = lax.broadcasted_iota(jnp.int32, (t_len, gp, ncp), 2)
    mask = (c * CMP_STRIDE + CMP_LEN - 1 <= qpos) & (c < nc)
    qpos_blk = past_len + lax.broadcasted_iota(jnp.int32, (t_len, ov_ref.shape[1]), 0)
    scores = []
    for b in range(bs):
        kvc = kvc_ref[b].astype(BF16)
        for g in range(N_KV):
            s = _dot_t(qz_ref[b, g], kvc[:, :LANES]).reshape(t_len, gp, ncp)
            p = _decode_softmax(s, mask)
            o_ref[b, g] = _dot(p.reshape(t_len * gp, ncp).astype(BF16), kvc[:, LANES:])
            scores.append(_block_scores(jnp.sum(p, axis=1), ov_ref, qpos_blk, ns))
    _, idx = _top_blocks(jnp.concatenate(scores, axis=0), min(N_SEL, ns))
    for b in range(bs):
        for g in range(N_KV):
            i = b * N_KV + g
            idx_ref[b, g] = idx[i * t_len:(i + 1) * t_len]


def _cmp_decode(qz, kvc, nc, ns, past_len, t_len, bs):
    batch, _, rows, _ = qz.shape
    ncp = kvc.shape[1]
    nsp = -(-ns // LANES) * LANES
    return pl.pallas_call(
        functools.partial(_cmp_decode_kernel, bs=bs, t_len=t_len, gp=rows // t_len, nc=nc, ns=ns, past_len=past_len),
        grid=(batch // bs,),
        in_specs=[pl.BlockSpec((bs, N_KV, rows, LANES), lambda i: (i, 0, 0, 0)),
                  pl.BlockSpec((bs, ncp, ROW_W), lambda i: (i, 0, 0)),
                  pl.BlockSpec((ncp, nsp), lambda i: (0, 0))],
        out_specs=[pl.BlockSpec((bs, N_KV, rows, LANES), lambda i: (i, 0, 0, 0)),
                   pl.BlockSpec((bs, N_KV, t_len, LANES), lambda i: (i, 0, 0, 0))],
        out_shape=[jax.ShapeDtypeStruct((batch, N_KV, rows, LANES), F32),
                   jax.ShapeDtypeStruct((batch, N_KV, t_len, LANES), jnp.int32)],
        compiler_params=_params(("parallel",)),
    )(qz, kvc, _overlap(ncp, nsp))


SLC_RING = 3


def _slc_decode_kernel(idx_ref, tab_ref, q_ref, pool_ref, new_ref, o_ref, kbuf, sem, *, n_prob, t_len, gp, n_sel,
                       past_len):
    b = pl.program_id(0)
    pairs = n_prob // 2
    total_pairs = pl.num_programs(0) * pairs
    per_page = PAGE_SIZE // SLC_BLK
    past_blocks = past_len // SLC_BLK

    def block_copy(ref, page, g, sl, k):
        return pltpu.make_async_copy(ref.at[page, :, g], kbuf.at[sl, :, :, pl.ds(k * PAGE_SIZE, PAGE_SIZE)],
                                     sem.at[sl])

    def issue(bb, i, sl):
        g = i // t_len if isinstance(i, int) else lax.div(i, t_len)
        for k in range(n_sel):
            blk = idx_ref[bb, i * n_sel + k]
            page = tab_ref[bb, lax.div(jnp.minimum(blk, past_blocks - 1), per_page)]
            lax.cond(blk < past_blocks,
                     lambda: block_copy(pool_ref, page, g, sl, k).start(),
                     lambda: block_copy(new_ref, bb, g, sl, k).start())

    def drain(sl):
        pltpu.make_async_copy(kbuf.at[sl], kbuf.at[sl], sem.at[sl]).wait()

    def issue_pair(bb, jp, base):
        for u in range(2):
            issue(bb, 2 * jp + u, base + u)

    @pl.when(b == 0)
    def _():
        for pair in range(SLC_RING - 1):
            issue_pair(pair // pairs, pair % pairs, (pair % SLC_RING) * 2)

    nk = n_sel * PAGE_SIZE
    lane = lax.broadcasted_iota(jnp.int32, (gp, nk), 1)
    row = lane % PAGE_SIZE

    def attend(i, sl):
        t = i % t_len
        blk = jnp.zeros((gp, nk), jnp.int32)
        for k in range(n_sel):
            blk = jnp.where(lane // PAGE_SIZE == k, idx_ref[b, i * n_sel + k], blk)
        kpos = (blk // per_page) * PAGE_SIZE + row
        mask = (kpos // SLC_BLK == blk) & (kpos <= past_len + t)
        rows = pl.ds(pl.multiple_of(i * gp, gp), gp)
        s = _dot(q_ref[rows, :].astype(BF16), kbuf[sl, 0].astype(BF16))
        p = _decode_softmax(s, mask)
        o_ref[rows, :] = _dot_t(p.astype(BF16), kbuf[sl, 1].astype(BF16))

    def pair_step(jp, carry):
        pair = b * pairs + jp
        base = lax.rem(pair, SLC_RING) * 2
        for u in range(2):
            drain(base + u)

        @pl.when(pair + SLC_RING - 1 < total_pairs)
        def _():
            ahead = jp + SLC_RING - 1
            wrap = jnp.where(ahead >= pairs, 1, 0)
            issue_pair(b + wrap, ahead - wrap * pairs, lax.rem(pair + SLC_RING - 1, SLC_RING) * 2)

        for u in range(2):
            attend(2 * jp + u, base + u)
        return carry

    lax.fori_loop(0, pairs, pair_step, 0)


def _slc_decode(idx, page_table, qh, pool_tiles, new_tiles, past_len):
    batch, _, t_len, n_sel = idx.shape
    assert t_len <= PAGE_SIZE, "new rows must fit one page-sized tile"
    rows = qh.shape[2]
    n_prob = N_KV * t_len
    slots = 2 * SLC_RING
    grid_spec = pltpu.PrefetchScalarGridSpec(
        num_scalar_prefetch=2,
        grid=(batch,),
        in_specs=[pl.BlockSpec((None, N_KV * rows, HEAD_DIM), lambda b, *_: (b, 0, 0)),
                  pl.BlockSpec(memory_space=pl.ANY), pl.BlockSpec(memory_space=pl.ANY)],
        out_specs=pl.BlockSpec((None, N_KV * rows, HEAD_DIM), lambda b, *_: (b, 0, 0)),
        scratch_shapes=[pltpu.VMEM((slots, 2, HEAD_DIM, n_sel * PAGE_SIZE), F32), pltpu.SemaphoreType.DMA((slots,))],
    )
    out = pl.pallas_call(
        functools.partial(_slc_decode_kernel, n_prob=n_prob, t_len=t_len, gp=rows // t_len, n_sel=n_sel,
                          past_len=past_len),
        grid_spec=grid_spec,
        out_shape=jax.ShapeDtypeStruct((batch, N_KV * rows, HEAD_DIM), F32),
        compiler_params=_params(("arbitrary",)),
    )(idx.reshape(batch, n_prob * n_sel), page_table, qh.reshape(batch, N_KV * rows, HEAD_DIM), pool_tiles, new_tiles)
    return out.reshape(batch, N_KV, rows, HEAD_DIM)


def _compress_kernel(tab_ref, pool_ref, perm_ref, w1p_ref, w1e_ref, pe_ref, b1_ref, w2_ref, out_ref, xt, xl, sem,
                     *, pg, nj, npg):
    b, j = pl.program_id(0), pl.program_id(1)
    n = b * nj + j
    slot = n % 2
    per_page = PAGE_SIZE // CMP_STRIDE
    n_pages = pg + 2
    n_chunks = n_pages * per_page
    c_out = pg * per_page

    def page_copy(bb, jj, sl, p):
        page = tab_ref[bb, jnp.minimum(jj * pg + p, npg - 1)]
        return pltpu.make_async_copy(pool_ref.at[page], xt.at[sl, p], sem.at[sl])

    def start(bb, jj, sl):
        lax.fori_loop(0, n_pages, lambda p, c: (page_copy(bb, jj, sl, p).start(), c)[1], 0)

    def wait(bb, jj, sl):
        lax.fori_loop(0, n_pages, lambda p, c: (page_copy(bb, jj, sl, p).wait(), c)[1], 0)

    @pl.when(n == 0)
    def _():
        start(b, j, slot)

    wait(b, j, slot)

    @pl.when(n + 1 < pl.num_programs(0) * nj)
    def _():
        nn = n + 1
        start(nn // nj, nn % nj, 1 - slot)

    def to_rows(q, c):
        both = jnp.concatenate([xt[slot, 2 * q], xt[slot, 2 * q + 1]], axis=1).astype(BF16)
        t = _dot_t(perm_ref[...], both).astype(BF16)
        for l in range(CMP_STRIDE):
            xl[l, pl.ds(pl.multiple_of(q * 2 * per_page, 2 * per_page), 2 * per_page), :] = (
                t[l * 2 * per_page:(l + 1) * 2 * per_page, :])
        return c

    lax.fori_loop(0, n_pages // 2, to_rows, 0, unroll=3)

    outs = []
    for s in range(2):
        window = jnp.concatenate([xl[l, :, s * LANES:(s + 1) * LANES] for l in range(CMP_STRIDE)], axis=1)
        acc = _dot(window, w1p_ref[s])
        bias = _dot(pe_ref[s], w1e_ref[s])[0:1] + b1_ref[s]
        hidden = []
        for k in range(N_KV):
            a = acc[:, k * LANES:(k + 1) * LANES]
            nxt = pltpu.roll(_swap_halves(a), n_chunks - 1, 0)
            hidden.append(jax.nn.gelu(a + nxt + bias).astype(BF16))
        outs.append(_dot(jnp.concatenate(hidden, axis=1), w2_ref[s]))
    out_ref[...] = jnp.concatenate(outs, axis=1)[:c_out]


def _compress(pool, table, cmp_w1, cmp_b1, cmp_pe, cmp_w2, pg):
    nb, npg = table.shape
    per_page = PAGE_SIZE // CMP_STRIDE
    assert pg % 2 == 0, "pages are transposed in pairs"
    r = jnp.arange(2 * PAGE_SIZE)
    chunk, offset = r % (2 * per_page), r // (2 * per_page)
    source = (chunk // per_page) * PAGE_SIZE + (chunk % per_page) * CMP_STRIDE + offset
    perm = (r[None, :] == source[:, None]).astype(BF16)
    nj = npg // pg
    hid = cmp_w1.shape[-1]
    w = cmp_w1.reshape(2, CMP_LEN // CMP_STRIDE, CMP_STRIDE, HEAD_DIM, hid)
    base = w.transpose(0, 2, 3, 1, 4).reshape(2, CMP_STRIDE, HEAD_DIM, 2 * hid)
    eye = jnp.eye(N_KV, dtype=F32)
    wsl = jnp.einsum("ab,sldn->sladbn", eye, base).reshape(2, CMP_STRIDE, LANES, ROW_W)
    w1p = wsl.reshape(2, CMP_STRIDE * LANES, ROW_W).astype(BF16)
    w1e = jnp.pad(cmp_w1, ((0, 0), (0, 0), (0, LANES - hid))).astype(BF16)
    pe = jnp.broadcast_to(cmp_pe.reshape(2, 1, CMP_LEN * HEAD_DIM), (2, 8, CMP_LEN * HEAD_DIM)).astype(BF16)
    b1 = jnp.pad(cmp_b1, ((0, 0), (0, LANES - hid))).reshape(2, 1, LANES)
    w2h = jnp.pad(cmp_w2, ((0, 0), (0, LANES - hid), (0, 0)))
    w2c = jnp.einsum("ab,shd->sahbd", eye, w2h).reshape(2, ROW_W, N_KV * HEAD_DIM).astype(BF16)
    c_out = pg * PAGE_SIZE // CMP_STRIDE
    fix3 = lambda b, j, tab: (0, 0, 0)
    grid_spec = pltpu.PrefetchScalarGridSpec(
        num_scalar_prefetch=1,
        grid=(nb, nj),
        in_specs=[pl.BlockSpec(memory_space=pl.ANY), pl.BlockSpec(perm.shape, lambda b, j, tab: (0, 0)),
                  pl.BlockSpec(w1p.shape, fix3), pl.BlockSpec(w1e.shape, fix3),
                  pl.BlockSpec(pe.shape, fix3), pl.BlockSpec(b1.shape, fix3), pl.BlockSpec(w2c.shape, fix3)],
        out_specs=pl.BlockSpec((None, c_out, ROW_W), lambda b, j, tab: (b, j, 0)),
        scratch_shapes=[pltpu.VMEM((2, pg + 2, ROW_W, PAGE_SIZE), F32),
                        pltpu.VMEM((CMP_STRIDE, (pg + 2) * per_page, ROW_W), BF16),
                        pltpu.SemaphoreType.DMA((2,))],
    )
    return pl.pallas_call(
        functools.partial(_compress_kernel, pg=pg, nj=nj, npg=npg),
        grid_spec=grid_spec,
        out_shape=jax.ShapeDtypeStruct((nb, npg * per_page, ROW_W), F32),
        compiler_params=_params(("arbitrary", "arbitrary")),
    )(table, pool, perm, w1p, w1e, pe, b1, w2c)


def _overlap(ncp, nsp):
    ci = jnp.arange(ncp)[:, None] * CMP_STRIDE
    sj = jnp.arange(nsp)[None, :] * SLC_BLK
    return ((ci < sj + SLC_BLK) & (ci + CMP_LEN > sj)).astype(BF16)


def _cmp_prompt(q, kvc, batch, s_len, nc, tq):
    nq = q.shape[1]
    m = batch * s_len
    nqt = s_len // tq
    gw = nq // N_KV
    ncp = kvc.shape[1]
    ns = s_len // SLC_BLK
    assert ns <= LANES, "selection blocks of a prompt must fit one lane tile"
    return pl.pallas_call(
        functools.partial(_cmp_prompt_kernel, tq=tq, gp=gw // HEAD_DIM, nc=nc, ns=ns),
        grid=(batch, N_KV, nqt),
        in_specs=[pl.BlockSpec((tq, gw), lambda b, g, i: (b * nqt + i, g)),
                  pl.BlockSpec((None, ncp, ROW_W), lambda b, g, i: (b, 0, 0)),
                  pl.BlockSpec((LANES, ncp), lambda b, g, i: (0, 0))],
        out_specs=[pl.BlockSpec((tq, gw), lambda b, g, i: (b * nqt + i, g)),
                   pl.BlockSpec((None, tq, LANES), lambda b, g, i: (g, b * nqt + i, 0))],
        out_shape=[jax.ShapeDtypeStruct((m, nq), F32), jax.ShapeDtypeStruct((N_KV, m, LANES), BF16)],
        compiler_params=_params(("parallel", "parallel", "parallel")),
    )(q, kvc, _overlap(ncp, LANES).T)


TM = 512
TF = 1408
TQ = 128
DEC_BS = 8
CMP_PG_SAMPLE = 64


def kernel(x_prompt, x_sample, cache_swa_a, cache_cmp, cache_slc, cache_win_b, page_table, a_norm1, a_wqkv, a_qnorm, a_knorm, a_sink, a_wo, a_norm2, a_w13, a_w2, kv_norm, kv_w, kv_knorm, cmp_pe, cmp_w1, cmp_b1, cmp_w2, b_norm1, b_wq, b_qnorm, b_wo, b_norm2, b_w13, b_w2):
    b_p, s_len, d = x_prompt.shape
    b_s, t_len, _ = x_sample.shape
    mp, ms = b_p * s_len, b_s * t_len
    n_pool = cache_cmp.shape[0]
    past_len = page_table.shape[1] * PAGE_SIZE
    assert t_len < CMP_STRIDE, "new tokens must not complete a compression chunk"
    pos = jnp.concatenate([jnp.tile(jnp.arange(s_len, dtype=jnp.int32), b_p),
                           jnp.tile(past_len + jnp.arange(t_len, dtype=jnp.int32), b_s)])
    tabs = _rot_tables(pos)
    h = jnp.concatenate([x_prompt.reshape(mp, d), x_sample.reshape(ms, d)], axis=0)
    row5 = lambda r, b: r.reshape(b, -1, 2, N_KV, HEAD_DIM)

    swa_p, swa_s = [], []
    for l in range(a_norm1.shape[0]):
        q, rows = _project_a(h, a_norm1[l], a_wqkv[l], a_qnorm[l], a_knorm[l], tabs, TM)
        rows_p, rows_s = rows[:mp], rows[mp:]
        o_p = _flash_prompt(q, rows, b_p, s_len, "band", WIN_A, TQ, 128, sink=a_sink[l])
        wa = cache_swa_a.shape[2]
        o_h = _window_decode(_to_qh(q[mp:], b_s), _tiles(cache_swa_a[l]), _new_tiles(rows_s, b_s), WIN_A, t_len,
                             DEC_BS, sink=a_sink[l])
        h = _mix([(o_p, _from_qh(o_h, t_len))], a_wo[l].astype(BF16), h, TM)
        swa_p.append(row5(rows_p, b_p)[:, -min(WIN_A, s_len):])
        swa_s.append(jnp.concatenate([cache_swa_a[l], row5(rows_s, b_s)], axis=1)[:, -wa:])
        h = _swiglu(h, a_norm2[l], a_w13[l].astype(BF16), a_w2[l].astype(BF16), TM, TF)

    cmp_rows, slc_rows, win_rows, cmp_t, slc_t, win_t = _project_kv(h, kv_norm, kv_w, kv_knorm, tabs, TM, b_p, s_len)
    cmp_s, slc_s, win_s = cmp_rows[mp:], slc_rows[mp:], win_rows[mp:]
    pages_p = s_len // PAGE_SIZE
    pool_p = cmp_t.reshape(b_p, ROW_W, pages_p, PAGE_SIZE).transpose(0, 2, 1, 3).reshape(b_p * pages_p, ROW_W, PAGE_SIZE)
    kvc_p = _compress(pool_p, jnp.arange(b_p * pages_p, dtype=jnp.int32).reshape(b_p, pages_p),
                      cmp_w1, cmp_b1, cmp_pe, cmp_w2, pages_p)
    kvc_s = _compress(_tiles(cache_cmp).reshape(n_pool, ROW_W, PAGE_SIZE), page_table, cmp_w1, cmp_b1, cmp_pe,
                      cmp_w2, CMP_PG_SAMPLE)
    nc_p = s_len // CMP_STRIDE - CMP_LEN // CMP_STRIDE + 1
    nc_s = past_len // CMP_STRIDE - CMP_LEN // CMP_STRIDE + 1
    ns_s = -(-(past_len + t_len) // SLC_BLK)
    wb = cache_win_b.shape[1]

    for j in range(b_norm1.shape[0]):
        q, gates = _project_bq(h, b_norm1[j], b_wq[j], b_qnorm[j], tabs, TM)
        oc_p, sel = _cmp_prompt(q, kvc_p, b_p, s_len, nc_p, 2 * TQ)
        os_p = _flash_prompt(q, slc_rows, b_p, s_len, "slc", 0, TQ, 256, sel=sel)
        ow_p = _flash_prompt(q, win_rows, b_p, s_len, "band", WIN_B, TQ, 256)
        qh = _to_qh(q[mp:], b_s)
        oc_z, idx = _cmp_decode(_to_qz(q[mp:], b_s), kvc_s, nc_s, ns_s, past_len, t_len, DEC_BS // 2)
        os_h = _slc_decode(idx[..., :min(N_SEL, ns_s)], page_table, qh, _tiles(cache_slc), _new_tiles(slc_s, b_s),
                           past_len)
        ow_h = _window_decode(qh, _tiles(cache_win_b), _new_tiles(win_s, b_s), WIN_B, t_len, DEC_BS)
        h = _mix([(oc_p, _from_qz(oc_z, t_len)), (os_p, _from_qh(os_h, t_len)), (ow_p, _from_qh(ow_h, t_len))],
                 b_wo[j].astype(BF16), h, TM, gates=gates)
        h = _swiglu(h, b_norm2[j], b_w13[j].astype(BF16), b_w2[j].astype(BF16), TM, TF)

    hp = h[:mp].reshape(b_p, s_len, d)
    hs = h[mp:].reshape(b_s, t_len, d)
    from_tiles = lambda t: t.reshape(b_p, 2, N_KV, HEAD_DIM, -1).transpose(0, 4, 1, 2, 3)
    win_b_prompt = from_tiles(win_t[:, :, s_len - min(WIN_B, s_len):])
    win_b_sample = jnp.concatenate([cache_win_b, row5(win_s, b_s)], axis=1)[:, -wb:]
    return (hp, hs, jnp.stack(swa_p, axis=0), jnp.stack(swa_s, axis=0),
            from_tiles(cmp_t), row5(cmp_s, b_s), from_tiles(slc_t), row5(slc_s, b_s), win_b_prompt, win_b_sample)
```

```python
import functools

import jax
import jax.numpy as jnp
from jax import lax
from jax.experimental import pallas as pl
from jax.experimental.pallas import tpu as pltpu

HEAD_DIM = 64
N_KV = 2
ROW_W = 2 * N_KV * HEAD_DIM
N_BRANCH = 3
WIN_A = 128
WIN_B = 512
CMP_LEN = 32
CMP_STRIDE = 16
SLC_BLK = 64
N_SEL = 16
PAGE_SIZE = 128
ROPE_THETA = 500000.0
ROT_DIM = HEAD_DIM // 4
EPS = 1e-6
FORCE_SCORE = 1e9
SCALE = HEAD_DIM ** -0.5
NEG = -1e30

LANES = 128
VMEM_LIMIT = 56 * 1024 * 1024

BF16 = jnp.bfloat16
F32 = jnp.float32


def _params(sem):
    return pltpu.CompilerParams(dimension_semantics=sem, vmem_limit_bytes=VMEM_LIMIT)


def _dot(a, b):
    return jnp.dot(a, b, preferred_element_type=F32)


def _dot_t(a, b):
    return lax.dot_general(a, b, (((1,), (1,)), ((), ())), preferred_element_type=F32)


def _rms(x, g):
    return x * lax.rsqrt(jnp.mean(x * x, axis=-1, keepdims=True) + EPS) * g


def _proj_kernel(x_ref, g_ref, w_ref, hg_ref, bd_ref, c_ref, s1_ref, s2_ref, *out_refs, plan, n_tiled_steps):
    xn = _rms(x_ref[...], g_ref[...]).astype(BF16)
    y = _dot(xn, w_ref[...])
    cos, s1, s2 = c_ref[...], s1_ref[...], s2_ref[...]
    for blk, (kind, out_idx, out_blk, scale, tile_idx) in enumerate(plan):
        lanes = slice(blk * LANES, (blk + 1) * LANES)
        t = y[:, lanes]
        if kind == "rot":
            ms = _dot((t * t).astype(BF16), bd_ref[...])
            t = t * lax.rsqrt(ms + EPS) * hg_ref[:, lanes]
            t = t * cos + pltpu.roll(t, LANES - ROT_DIM // 2, 1) * s1 + pltpu.roll(t, ROT_DIM // 2, 1) * s2
        elif kind == "sig":
            t = jax.nn.sigmoid(t)
        if scale != 1.0:
            t = t * scale
        out_refs[out_idx][:, out_blk * LANES:(out_blk + 1) * LANES] = t
        if tile_idx is not None:
            @pl.when(pl.program_id(0) < n_tiled_steps)
            def _():
                out_refs[tile_idx][out_blk * LANES:(out_blk + 1) * LANES, :] = t.T


def _proj(x, g, w, hg, tabs, plan, out_widths, tm, tiles=None):
    m, d = x.shape
    n = w.shape[1]
    bd = jnp.kron(jnp.eye(LANES // HEAD_DIM, dtype=F32), jnp.full((HEAD_DIM, HEAD_DIM), 1.0 / HEAD_DIM, F32)).astype(BF16)
    row = lambda i: (i, 0)
    fix = lambda i: (0, 0)
    out_specs = [pl.BlockSpec((tm, wd), row) for wd in out_widths]
    out_shape = [jax.ShapeDtypeStruct((m, wd), F32) for wd in out_widths]
    n_p = 0
    if tiles is not None:
        n_tiled, batch, s_len = tiles
        per_seq = s_len // tm
        n_p = batch * per_seq
        tile_map = lambda i: (jnp.minimum(i // per_seq, batch - 1), 0, jnp.where(i < n_p, i % per_seq, per_seq - 1))
        out_specs += [pl.BlockSpec((None, ROW_W, tm), tile_map)] * n_tiled
        out_shape += [jax.ShapeDtypeStruct((batch, ROW_W, s_len), F32)] * n_tiled
    return pl.pallas_call(
        functools.partial(_proj_kernel, plan=tuple(plan), n_tiled_steps=n_p),
        grid=(m // tm,),
        in_specs=[pl.BlockSpec((tm, d), row), pl.BlockSpec((1, d), fix), pl.BlockSpec((d, n), fix),
                  pl.BlockSpec((1, n), fix), pl.BlockSpec((LANES, LANES), fix),
                  pl.BlockSpec((tm, LANES), row), pl.BlockSpec((tm, LANES), row), pl.BlockSpec((tm, LANES), row)],
        out_specs=out_specs,
        out_shape=out_shape,
        compiler_params=_params(("arbitrary",)),
    )(x, g.reshape(1, d), w, hg.reshape(1, n), bd, *tabs)


def _mix_kernel(*refs, n_br, n_p, gated):
    pairs = [refs[2 * br:2 * br + 2] for br in range(n_br)]
    rest = refs[2 * n_br:]
    if gated:
        gt_ref, e_ref, w_ref, r_ref, out_ref = rest
        gt = gt_ref[...]
        g_hi = gt.astype(BF16)
        g_lo = (gt - g_hi.astype(F32)).astype(BF16)
    else:
        w_ref, r_ref, out_ref = rest
    in_prompt = pl.program_id(0) < n_p
    o = None
    for br, (p_ref, s_ref) in enumerate(pairs):
        term = jnp.where(in_prompt, p_ref[...], s_ref[...])
        if gated:
            term = term * (_dot(g_hi, e_ref[br]) + _dot(g_lo, e_ref[br]))
        o = term if o is None else o + term
    out_ref[...] = r_ref[...] + _dot(o.astype(BF16), w_ref[...])


def _mix(branches, w, res, tm, gates=None):
    m, d = res.shape
    mp, k = branches[0][0].shape
    n_p = mp // tm
    n_br = len(branches)
    row = lambda i: (i, 0)
    fix = lambda i: (0, 0)
    in_specs, args = [], []
    for o_p, o_s in branches:
        in_specs += [pl.BlockSpec((tm, k), lambda i: (jnp.minimum(i, n_p - 1), 0)),
                     pl.BlockSpec((tm, k), lambda i: (jnp.maximum(i - n_p, 0), 0))]
        args += [o_p, o_s]
    if gates is not None:
        n_heads = k // HEAD_DIM
        expand = (jnp.arange(LANES)[None, :, None] == (jnp.arange(n_br)[:, None, None] * n_heads
                                                       + jnp.arange(k)[None, None, :] // HEAD_DIM)).astype(BF16)
        in_specs += [pl.BlockSpec((tm, LANES), row), pl.BlockSpec((n_br, LANES, k), lambda i: (0, 0, 0))]
        args += [gates, expand]
    in_specs += [pl.BlockSpec((k, d), fix), pl.BlockSpec((tm, d), row)]
    args += [w, res]
    return pl.pallas_call(
        functools.partial(_mix_kernel, n_br=n_br, n_p=n_p, gated=gates is not None),
        grid=(m // tm,),
        in_specs=in_specs,
        out_specs=pl.BlockSpec((tm, d), row),
        out_shape=jax.ShapeDtypeStruct((m, d), F32),
        compiler_params=_params(("parallel",)),
    )(*args)


def _swiglu_kernel(h_ref, g_ref, wg_ref, wu_ref, w2_ref, out_ref, xn_ref, acc_ref):
    f = pl.program_id(1)

    @pl.when(f == 0)
    def _():
        xn_ref[...] = _rms(h_ref[...], g_ref[...]).astype(BF16)
        acc_ref[...] = h_ref[...]

    xn = xn_ref[...]
    gate = _dot(xn, wg_ref[...])
    up = _dot(xn, wu_ref[...])
    acc_ref[...] += _dot((jax.nn.silu(gate) * up).astype(BF16), w2_ref[...])

    @pl.when(f == pl.num_programs(1) - 1)
    def _():
        out_ref[...] = acc_ref[...]


def _swiglu(h, g, w13, w2, tm, tf):
    m, d = h.shape
    dff = w2.shape[0]
    nf = dff // tf
    return pl.pallas_call(
        _swiglu_kernel,
        grid=(m // tm, nf),
        in_specs=[pl.BlockSpec((tm, d), lambda i, f: (i, 0)), pl.BlockSpec((1, d), lambda i, f: (0, 0)),
                  pl.BlockSpec((d, tf), lambda i, f: (0, f)), pl.BlockSpec((d, tf), lambda i, f: (0, nf + f)),
                  pl.BlockSpec((tf, d), lambda i, f: (f, 0))],
        out_specs=pl.BlockSpec((tm, d), lambda i, f: (i, 0)),
        out_shape=jax.ShapeDtypeStruct((m, d), F32),
        scratch_shapes=[pltpu.VMEM((tm, d), BF16), pltpu.VMEM((tm, d), F32)],
        compiler_params=_params(("parallel", "arbitrary")),
    )(h, g.reshape(1, d), w13, w13, w2)


def _rot_tables(pos):
    half = ROT_DIM // 2
    inv = jnp.power(ROPE_THETA, -jnp.arange(half, dtype=F32) * (2.0 / ROT_DIM))
    ang = pos.astype(F32)[:, None] * inv[None, :]
    cos, sin = jnp.cos(ang), jnp.sin(ang)
    m = pos.shape[0]
    zeros = lambda n: jnp.zeros((m, n), F32)
    c = jnp.concatenate([cos, cos, jnp.ones((m, HEAD_DIM - ROT_DIM), F32)], axis=1)
    s1 = jnp.concatenate([-sin, zeros(HEAD_DIM - half)], axis=1)
    s2 = jnp.concatenate([zeros(half), sin, zeros(HEAD_DIM - ROT_DIM)], axis=1)
    reps = LANES // HEAD_DIM
    return tuple(jnp.tile(t, (1, reps)) for t in (c, s1, s2))


def _project_a(h, norm1, wqkv, qn, kn, tabs, tm):
    nq = wqkv.shape[1] - ROW_W
    hg = jnp.concatenate([jnp.tile(qn, nq // HEAD_DIM), jnp.tile(kn, N_KV), jnp.ones((N_KV * HEAD_DIM,), F32)])
    plan = ([("rot", 0, c, SCALE, None) for c in range(nq // LANES)]
            + [("rot", 1, 0, 1.0, None), ("id", 1, 1, 1.0, None)])
    return _proj(h, norm1, wqkv.astype(BF16), hg, tabs, plan, (nq, ROW_W), tm)


def _project_kv(h, kv_norm, kv_w, kv_knorm, tabs, tm, batch, s_len):
    hg = jnp.concatenate([jnp.concatenate([jnp.tile(kv_knorm[br], N_KV), jnp.ones((N_KV * HEAD_DIM,), F32)])
                          for br in range(N_BRANCH)])
    plan = []
    for br in range(N_BRANCH):
        plan += [("rot", br, 0, 1.0, N_BRANCH + br), ("id", br, 1, 1.0, N_BRANCH + br)]
    return _proj(h, kv_norm, kv_w.astype(BF16), hg, tabs, plan, (ROW_W,) * N_BRANCH, tm,
                 tiles=(N_BRANCH, batch, s_len))


def _project_bq(h, norm1, wq, qn, tabs, tm):
    n_gate = wq.shape[1] % LANES
    nq = wq.shape[1] - n_gate
    wq = jnp.pad(wq, ((0, 0), (0, LANES - n_gate))).astype(BF16)
    hg = jnp.concatenate([jnp.tile(qn, nq // HEAD_DIM), jnp.ones((LANES,), F32)])
    plan = [("rot", 0, c, SCALE, None) for c in range(nq // LANES)] + [("sig", 1, 0, 1.0, None)]
    return _proj(h, norm1, wq, hg, tabs, plan, (nq, LANES), tm)


def _swap_halves(x):
    return pltpu.roll(x, HEAD_DIM, x.ndim - 1)


def _stack_heads(q, g):
    t, width = q.shape
    lane = lax.broadcasted_iota(jnp.int32, (t, LANES), 1)
    on_group = (lane >= HEAD_DIM) == (g == 1)
    parts = []
    for h in range(width // HEAD_DIM):
        src = q[:, (h // 2) * LANES:(h // 2 + 1) * LANES]
        own = (lane >= HEAD_DIM) == (h % 2 == 1)
        both = jnp.where(own, src, _swap_halves(src))
        parts.append(jnp.where(on_group, both, 0.0).astype(BF16))
    return jnp.concatenate(parts, axis=0)


def _unstack_heads(o, g):
    gp, t, _ = o.shape
    lane = lax.broadcasted_iota(jnp.int32, (t, LANES), 1)
    blocks = []
    for j in range(gp // 2):
        halves = []
        for h in (2 * j, 2 * j + 1):
            halves.append(jnp.where(g == h % 2, o[h], _swap_halves(o[h])))
        blocks.append(jnp.where(lane < HEAD_DIM, halves[0], halves[1]))
    return jnp.concatenate(blocks, axis=1)


def _flash_kernel(*refs, mode, window, tq, tk, gp, has_sink):
    if mode == "slc":
        q_ref, kv_ref, sel_ref, expand_ref, *rest = refs
    else:
        q_ref, kv_ref, *rest = refs
    if has_sink:
        sink_ref, *rest = rest
    o_ref, m_scr, acc_scr = rest
    g = pl.program_id(1)
    q0 = pl.program_id(2) * tq
    qz = _stack_heads(q_ref[...], g)
    m_scr[...] = jnp.full(m_scr.shape, NEG, F32)
    acc_scr[...] = jnp.zeros(acc_scr.shape, F32)
    qpos = q0 + lax.broadcasted_iota(jnp.int32, (tq, tk), 0)
    lo = jnp.maximum(q0 - (window - 1), 0) // tk if mode == "band" else 0
    hi = (q0 + tq - 1) // tk + 1
    reps = tk // LANES
    sum_lanes = (lax.broadcasted_iota(jnp.int32, (tk, LANES), 1) >= HEAD_DIM) != (g == 1)

    def body(kt, carry):
        k0 = pl.multiple_of(kt * tk, tk)
        kvt = kv_ref[pl.ds(k0, tk), :].astype(BF16)
        vt = jnp.where(sum_lanes, 1.0, kvt[:, LANES:])
        s_all = _dot_t(qz, kvt[:, :LANES])
        diff = qpos - (k0 + lax.broadcasted_iota(jnp.int32, (tq, tk), 1))
        if mode == "band":
            mask = (diff >= 0) & (diff < window)
        else:
            mask = (diff >= 0) & (_dot(sel_ref[...], expand_ref[kt]) > 0.5)
        ps = []
        for h in range(gp):
            s = jnp.where(mask, s_all[h * tq:(h + 1) * tq], NEG)
            m_prev = m_scr[h]
            m_new = jnp.maximum(m_prev, jnp.max(s, axis=-1, keepdims=True))
            alpha = jnp.exp(m_prev - m_new)
            ps.append(jnp.exp((s - jnp.concatenate([m_new] * reps, axis=1)).astype(BF16)))
            acc_scr[h] = alpha * acc_scr[h]
            m_scr[h] = m_new
        pv = _dot(jnp.concatenate(ps, axis=0), vt)
        for h in range(gp):
            acc_scr[h] += pv[h * tq:(h + 1) * tq]
        return carry

    lax.fori_loop(lo, hi, body, 0)
    m, acc = m_scr[...], acc_scr[...]
    l = _swap_halves(acc.reshape(gp * tq, LANES)).reshape(gp, tq, LANES)
    if has_sink:
        sink = jnp.concatenate([jnp.full((1, tq, LANES), sink_ref[g * gp + h], F32) for h in range(gp)], axis=0)
        m_fin = jnp.maximum(m, sink)
        scale = jnp.exp(m - m_fin)
        l = l * scale + jnp.exp(sink - m_fin)
        acc = acc * scale
    o_ref[...] = _unstack_heads(acc * (1.0 / l), g)


def _flash_prompt(q, rows, batch, s_len, mode, window, tq, tk, sel=None, sink=None):
    nq = q.shape[1]
    m = batch * s_len
    nqt = s_len // tq
    gw = nq // N_KV
    gp = gw // HEAD_DIM
    in_specs = [pl.BlockSpec((tq, gw), lambda b, g, i: (b * nqt + i, g)),
                pl.BlockSpec((s_len, ROW_W), lambda b, g, i: (b, 0))]
    args = [q, rows]
    if mode == "slc":
        key_blk = (jnp.arange(s_len) // SLC_BLK).reshape(s_len // tk, 1, tk)
        expand = (key_blk == jnp.arange(LANES)[None, :, None]).astype(BF16)
        in_specs += [pl.BlockSpec((None, tq, LANES), lambda b, g, i: (g, b * nqt + i, 0)),
                     pl.BlockSpec(expand.shape, lambda b, g, i: (0, 0, 0))]
        args += [sel, expand]
    if sink is not None:
        in_specs.append(pl.BlockSpec(memory_space=pltpu.SMEM))
        args.append(sink)
    return pl.pallas_call(
        functools.partial(_flash_kernel, mode=mode, window=window, tq=tq, tk=tk, gp=gp, has_sink=sink is not None),
        grid=(batch, N_KV, nqt),
        in_specs=in_specs,
        out_specs=pl.BlockSpec((tq, gw), lambda b, g, i: (b * nqt + i, g)),
        out_shape=jax.ShapeDtypeStruct((m, nq), F32),
        scratch_shapes=[pltpu.VMEM((gp, tq, LANES), F32)] * 2,
        compiler_params=_params(("parallel", "parallel", "arbitrary")),
    )(*args)


def _top_blocks(score, n_sel):
    t, w = score.shape
    lane = lax.broadcasted_iota(jnp.int32, (t, w), 1)
    picked = jnp.zeros((t, w), F32)
    idx = jnp.zeros((t, LANES), jnp.int32)
    idx_lane = lax.broadcasted_iota(jnp.int32, (t, LANES), 1)
    for it in range(n_sel):
        top = jnp.max(score, axis=-1, keepdims=True)
        first = jnp.min(jnp.where(score == top, lane, w), axis=-1, keepdims=True)
        hit = lane == first
        picked = jnp.where(hit, 1.0, picked)
        score = jnp.where(hit, -jnp.inf, score)
        idx = jnp.where(idx_lane == it, first, idx)
    return picked, idx


def _pick_by_rank(score_t, n_sel, ns):
    blk = lax.broadcasted_iota(jnp.int32, score_t.shape, 0)
    beaten = jnp.zeros(score_t.shape, jnp.int32)
    for j in range(ns):
        row = jnp.broadcast_to(score_t[j:j + 1, :], score_t.shape)
        beaten = beaten + jnp.where((row > score_t) | ((row == score_t) & (blk > j)), 1, 0)
    return jnp.where((beaten < n_sel) & (blk < ns), 1.0, 0.0)


def _block_scores_t(psum, overlap_t_ref, qpos_t, ns):
    p_hi = psum.astype(BF16)
    p_lo = (psum - p_hi.astype(F32)).astype(BF16)
    imp = _dot_t(overlap_t_ref[...], p_hi) + _dot_t(overlap_t_ref[...], p_lo)
    j = lax.broadcasted_iota(jnp.int32, imp.shape, 0)
    cur = qpos_t // SLC_BLK
    forced = (j == 0) | (j == cur) | (j == cur - 1)
    score = jnp.where(forced, FORCE_SCORE, jnp.where(j <= cur, imp, -1.0))
    return jnp.where(j < ns, score, -jnp.inf)


def _block_scores(psum, overlap_ref, qpos, ns):
    p_hi = psum.astype(BF16)
    p_lo = (psum - p_hi.astype(F32)).astype(BF16)
    imp = _dot(p_hi, overlap_ref[...]) + _dot(p_lo, overlap_ref[...])
    j = lax.broadcasted_iota(jnp.int32, imp.shape, 1)
    cur = qpos // SLC_BLK
    forced = (j == 0) | (j == cur) | (j == cur - 1)
    score = jnp.where(forced, FORCE_SCORE, jnp.where(j <= cur, imp, -1.0))
    return jnp.where(j < ns, score, -jnp.inf)


def _cmp_prompt_kernel(q_ref, kvc_ref, ov_ref, o_ref, sel_ref, *, tq, gp, nc, ns):
    g = pl.program_id(1)
    q0 = pl.program_id(2) * tq
    qz = _stack_heads(q_ref[...], g)
    kvc = kvc_ref[...].astype(BF16)
    ncp = kvc.shape[0]
    s_all = _dot_t(qz, kvc[:, :LANES])
    qpos = q0 + lax.broadcasted_iota(jnp.int32, (tq, ncp), 0)
    c = lax.broadcasted_iota(jnp.int32, (tq, ncp), 1)
    mask = (c * CMP_STRIDE + CMP_LEN - 1 <= qpos) & (c < nc)
    ps, psum = [], None
    for h in range(gp):
        p = _decode_softmax(s_all[h * tq:(h + 1) * tq], mask)
        psum = p if psum is None else psum + p
        ps.append(p.astype(BF16))
    o = _dot(jnp.concatenate(ps, axis=0), kvc[:, LANES:]).reshape(gp, tq, LANES)
    o_ref[...] = _unstack_heads(o, g)
    score_t = _block_scores_t(psum, ov_ref, q0 + lax.broadcasted_iota(jnp.int32, (LANES, tq), 1), ns)
    ns_rows = -(-ns // 8) * 8
    picks_t = _pick_by_rank(score_t[:ns_rows], min(N_SEL, ns), ns)
    picks_t = jnp.concatenate([picks_t, jnp.zeros((LANES - ns_rows, tq), F32)], axis=0)
    sel_ref[...] = picks_t.T.astype(BF16)


def _to_qz(q, batch):
    m, nq = q.shape
    t_len = m // batch
    gp = nq // (N_KV * HEAD_DIM)
    q5 = q.reshape(batch, t_len, N_KV, gp, HEAD_DIM).transpose(0, 2, 1, 3, 4)
    zero = jnp.zeros_like(q5[:, 0])
    qz = jnp.stack([jnp.concatenate([q5[:, 0], zero], axis=-1), jnp.concatenate([zero, q5[:, 1]], axis=-1)], axis=1)
    return qz.reshape(batch, N_KV, t_len * gp, LANES).astype(BF16)


def _from_qz(o, t_len):
    batch, _, rows, _ = o.shape
    gp = rows // t_len
    o6 = o.reshape(batch, N_KV, t_len, gp, LANES // HEAD_DIM, HEAD_DIM)
    x = jnp.stack([o6[:, g, :, :, g, :] for g in range(N_KV)], axis=2)
    return x.reshape(batch * t_len, N_KV * gp * HEAD_DIM)


def _decode_softmax(s, mask, sink=None):
    m = jnp.max(jnp.where(mask, s, NEG), axis=-1, keepdims=True)
    if sink is not None:
        m = jnp.maximum(m, sink)
    e = jnp.where(mask, jnp.exp(s - m), 0.0)
    den = jnp.sum(e, axis=-1, keepdims=True)
    if sink is not None:
        den = den + jnp.exp(sink - m)
    return e * (1.0 / jnp.where(den > 0, den, 1.0))


def _tiles(cache):
    nd = cache.ndim
    return jnp.transpose(cache, tuple(range(nd - 4)) + (nd - 3, nd - 2, nd - 1, nd - 4))


def _new_tiles(rows_new, batch):
    t_len = rows_new.shape[0] // batch
    t5 = _tiles(rows_new.reshape(batch, t_len, 2, N_KV, HEAD_DIM))
    return jnp.pad(t5, ((0, 0),) * 4 + ((0, LANES - t_len),))


def _to_qh(q, batch):
    m, nq = q.shape
    t_len = m // batch
    gp = nq // (N_KV * HEAD_DIM)
    return q.reshape(batch, t_len, N_KV, gp, HEAD_DIM).transpose(0, 2, 1, 3, 4).reshape(batch, N_KV, t_len * gp, HEAD_DIM)


def _from_qh(o, t_len):
    batch, _, rows, _ = o.shape
    gp = rows // t_len
    x = o.reshape(batch, N_KV, t_len, gp, HEAD_DIM).transpose(0, 2, 1, 3, 4)
    return x.reshape(batch * t_len, N_KV * gp * HEAD_DIM)


def _window_decode_kernel(*refs, bs, t_len, gp, window, has_sink):
    if has_sink:
        q_ref, buf_ref, new_ref, sink_ref, o_ref = refs
    else:
        q_ref, buf_ref, new_ref, o_ref = refs
    wb = buf_ref.shape[-1]
    nk = wb + LANES
    diff = (lax.broadcasted_iota(jnp.int32, (t_len, gp, nk), 0) + wb
            - lax.broadcasted_iota(jnp.int32, (t_len, gp, nk), 2))
    mask = (diff >= 0) & (diff < window)
    for b in range(bs):
        for g in range(N_KV):
            kt = jnp.concatenate([buf_ref[b, 0, g], new_ref[b, 0, g]], axis=1).astype(BF16)
            vt = jnp.concatenate([buf_ref[b, 1, g], new_ref[b, 1, g]], axis=1).astype(BF16)
            s = _dot(q_ref[b, g].astype(BF16), kt).reshape(t_len, gp, nk)
            sink = sink_ref[g].reshape(1, gp, 1) if has_sink else None
            p = _decode_softmax(s, mask, sink)
            o_ref[b, g] = _dot_t(p.reshape(t_len * gp, nk).astype(BF16), vt)


def _window_decode(qh, buf_t, new_t, window, t_len, bs, sink=None):
    batch, _, rows, _ = qh.shape
    wb = buf_t.shape[-1]
    gp = rows // t_len
    in_specs = [pl.BlockSpec((bs, N_KV, rows, HEAD_DIM), lambda i: (i, 0, 0, 0)),
                pl.BlockSpec((bs, 2, N_KV, HEAD_DIM, wb), lambda i: (i, 0, 0, 0, 0)),
                pl.BlockSpec((bs, 2, N_KV, HEAD_DIM, LANES), lambda i: (i, 0, 0, 0, 0))]
    args = [qh, buf_t, new_t]
    if sink is not None:
        in_specs.append(pl.BlockSpec((N_KV, gp, 1), lambda i: (0, 0, 0)))
        args.append(sink.reshape(N_KV, gp, 1))
    return pl.pallas_call(
        functools.partial(_window_decode_kernel, bs=bs, t_len=t_len, gp=gp, window=window, has_sink=sink is not None),
        grid=(batch // bs,),
        in_specs=in_specs,
        out_specs=pl.BlockSpec((bs, N_KV, rows, HEAD_DIM), lambda i: (i, 0, 0, 0)),
        out_shape=jax.ShapeDtypeStruct((batch, N_KV, rows, HEAD_DIM), F32),
        compiler_params=_params(("parallel",)),
    )(*args)


def _cmp_decode_kernel(qz_ref, kvc_ref, ov_ref, o_ref, idx_ref, *, bs, t_len, gp, nc, ns, past_len):
    ncp = kvc_ref.shape[1]
    qpos = past_len + lax.broadcasted_iota(jnp.int32, (t_len, gp, ncp), 0)
    c = lax.broadcasted_iota(jnp.int32, (t_len, gp, ncp), 2)
    mask = (c * CMP_STRIDE + CMP_LEN - 1 <= qpos) & (c < nc)
    qpos_blk = past_len + lax.broadcasted_iota(jnp.int32, (t_len, ov_ref.shape[1]), 0)
    scores = []
    for b in range(bs):
        kvc = kvc_ref[b].astype(BF16)
        for g in range(N_KV):
            s = _dot_t(qz_ref[b, g], kvc[:, :LANES]).reshape(t_len, gp, ncp)
            p = _decode_softmax(s, mask)
            o_ref[b, g] = _dot(p.reshape(t_len * gp, ncp).astype(BF16), kvc[:, LANES:])
            scores.append(_block_scores(jnp.sum(p, axis=1), ov_ref, qpos_blk, ns))
    _, idx = _top_blocks(jnp.concatenate(scores, axis=0), min(N_SEL, ns))
    for b in range(bs):
        for g in range(N_KV):
            i = b * N_KV + g
            idx_ref[b, g] = idx[i * t_len:(i + 1) * t_len]


def _cmp_decode(qz, kvc, nc, ns, past_len, t_len, bs):
    batch, _, rows, _ = qz.shape
    ncp = kvc.shape[1]
    nsp = -(-ns // LANES) * LANES
    return pl.pallas_call(
        functools.partial(_cmp_decode_kernel, bs=bs, t_len=t_len, gp=rows // t_len, nc=nc, ns=ns, past_len=past_len),
        grid=(batch // bs,),
        in_specs=[pl.BlockSpec((bs, N_KV, rows, LANES), lambda i: (i, 0, 0, 0)),
                  pl.BlockSpec((bs, ncp, ROW_W), lambda i: (i, 0, 0)),
                  pl.BlockSpec((ncp, nsp), lambda i: (0, 0))],
        out_specs=[pl.BlockSpec((bs, N_KV, rows, LANES), lambda i: (i, 0, 0, 0)),
                   pl.BlockSpec((bs, N_KV, t_len, LANES), lambda i: (i, 0, 0, 0))],
        out_shape=[jax.ShapeDtypeStruct((batch, N_KV, rows, LANES), F32),
                   jax.ShapeDtypeStruct((batch, N_KV, t_len, LANES), jnp.int32)],
        compiler_params=_params(("parallel",)),
    )(qz, kvc, _overlap(ncp, nsp))


SLC_RING = 3


def _slc_decode_kernel(idx_ref, tab_ref, q_ref, pool_ref, new_ref, o_ref, kbuf, sem, *, n_prob, t_len, gp, n_sel,
                       past_len):
    b = pl.program_id(0)
    pairs = n_prob // 2
    total_pairs = pl.num_programs(0) * pairs
    per_page = PAGE_SIZE // SLC_BLK
    past_blocks = past_len // SLC_BLK

    def block_copy(ref, page, g, sl, k):
        return pltpu.make_async_copy(ref.at[page, :, g], kbuf.at[sl, :, :, pl.ds(k * PAGE_SIZE, PAGE_SIZE)],
                                     sem.at[sl])

    def issue(bb, i, sl):
        g = i // t_len if isinstance(i, int) else lax.div(i, t_len)
        for k in range(n_sel):
            blk = idx_ref[bb, i * n_sel + k]
            page = tab_ref[bb, lax.div(jnp.minimum(blk, past_blocks - 1), per_page)]
            lax.cond(blk < past_blocks,
                     lambda: block_copy(pool_ref, page, g, sl, k).start(),
                     lambda: block_copy(new_ref, bb, g, sl, k).start())

    def drain(sl):
        pltpu.make_async_copy(kbuf.at[sl], kbuf.at[sl], sem.at[sl]).wait()

    def issue_pair(bb, jp, base):
        for u in range(2):
            issue(bb, 2 * jp + u, base + u)

    @pl.when(b == 0)
    def _():
        for pair in range(SLC_RING - 1):
            issue_pair(pair // pairs, pair % pairs, (pair % SLC_RING) * 2)

    nk = n_sel * PAGE_SIZE
    lane = lax.broadcasted_iota(jnp.int32, (gp, nk), 1)
    row = lane % PAGE_SIZE

    def attend(i, sl):
        t = i % t_len
        blk = jnp.zeros((gp, nk), jnp.int32)
        for k in range(n_sel):
            blk = jnp.where(lane // PAGE_SIZE == k, idx_ref[b, i * n_sel + k], blk)
        kpos = (blk // per_page) * PAGE_SIZE + row
        mask = (kpos // SLC_BLK == blk) & (kpos <= past_len + t)
        rows = pl.ds(pl.multiple_of(i * gp, gp), gp)
        s = _dot(q_ref[rows, :].astype(BF16), kbuf[sl, 0].astype(BF16))
        p = _decode_softmax(s, mask)
        o_ref[rows, :] = _dot_t(p.astype(BF16), kbuf[sl, 1].astype(BF16))

    def pair_step(jp, carry):
        pair = b * pairs + jp
        base = lax.rem(pair, SLC_RING) * 2
        for u in range(2):
            drain(base + u)

        @pl.when(pair + SLC_RING - 1 < total_pairs)
        def _():
            ahead = jp + SLC_RING - 1
            wrap = jnp.where(ahead >= pairs, 1, 0)
            issue_pair(b + wrap, ahead - wrap * pairs, lax.rem(pair + SLC_RING - 1, SLC_RING) * 2)

        for u in range(2):
            attend(2 * jp + u, base + u)
        return carry

    lax.fori_loop(0, pairs, pair_step, 0)


def _slc_decode(idx, page_table, qh, pool_tiles, new_tiles, past_len):
    batch, _, t_len, n_sel = idx.shape
    assert t_len <= PAGE_SIZE, "new rows must fit one page-sized tile"
    rows = qh.shape[2]
    n_prob = N_KV * t_len
    slots = 2 * SLC_RING
    grid_spec = pltpu.PrefetchScalarGridSpec(
        num_scalar_prefetch=2,
        grid=(batch,),
        in_specs=[pl.BlockSpec((None, N_KV * rows, HEAD_DIM), lambda b, *_: (b, 0, 0)),
                  pl.BlockSpec(memory_space=pl.ANY), pl.BlockSpec(memory_space=pl.ANY)],
        out_specs=pl.BlockSpec((None, N_KV * rows, HEAD_DIM), lambda b, *_: (b, 0, 0)),
        scratch_shapes=[pltpu.VMEM((slots, 2, HEAD_DIM, n_sel * PAGE_SIZE), F32), pltpu.SemaphoreType.DMA((slots,))],
    )
    out = pl.pallas_call(
        functools.partial(_slc_decode_kernel, n_prob=n_prob, t_len=t_len, gp=rows // t_len, n_sel=n_sel,
                          past_len=past_len),
        grid_spec=grid_spec,
        out_shape=jax.ShapeDtypeStruct((batch, N_KV * rows, HEAD_DIM), F32),
        compiler_params=_params(("arbitrary",)),
    )(idx.reshape(batch, n_prob * n_sel), page_table, qh.reshape(batch, N_KV * rows, HEAD_DIM), pool_tiles, new_tiles)
    return out.reshape(batch, N_KV, rows, HEAD_DIM)


def _compress_kernel(tab_ref, pool_ref, perm_ref, w1p_ref, w1e_ref, pe_ref, b1_ref, w2_ref, out_ref, xt, xl, sem,
                     *, pg, nj, npg):
    b, j = pl.program_id(0), pl.program_id(1)
    n = b * nj + j
    slot = n % 2
    per_page = PAGE_SIZE // CMP_STRIDE
    n_pages = pg + 2
    n_chunks = n_pages * per_page
    c_out = pg * per_page

    def page_copy(bb, jj, sl, p):
        page = tab_ref[bb, jnp.minimum(jj * pg + p, npg - 1)]
        return pltpu.make_async_copy(pool_ref.at[page], xt.at[sl, p], sem.at[sl])

    def start(bb, jj, sl):
        lax.fori_loop(0, n_pages, lambda p, c: (page_copy(bb, jj, sl, p).start(), c)[1], 0)

    def wait(bb, jj, sl):
        lax.fori_loop(0, n_pages, lambda p, c: (page_copy(bb, jj, sl, p).wait(), c)[1], 0)

    @pl.when(n == 0)
    def _():
        start(b, j, slot)

    wait(b, j, slot)

    @pl.when(n + 1 < pl.num_programs(0) * nj)
    def _():
        nn = n + 1
        start(nn // nj, nn % nj, 1 - slot)

    def to_rows(q, c):
        both = jnp.concatenate([xt[slot, 2 * q], xt[slot, 2 * q + 1]], axis=1).astype(BF16)
        t = _dot_t(perm_ref[...], both).astype(BF16)
        for l in range(CMP_STRIDE):
            xl[l, pl.ds(pl.multiple_of(q * 2 * per_page, 2 * per_page), 2 * per_page), :] = (
                t[l * 2 * per_page:(l + 1) * 2 * per_page, :])
        return c

    lax.fori_loop(0, n_pages // 2, to_rows, 0, unroll=3)

    outs = []
    for s in range(2):
        window = jnp.concatenate([xl[l, :, s * LANES:(s + 1) * LANES] for l in range(CMP_STRIDE)], axis=1)
        acc = _dot(window, w1p_ref[s])
        bias = _dot(pe_ref[s], w1e_ref[s])[0:1] + b1_ref[s]
        hidden = []
        for k in range(N_KV):
            a = acc[:, k * LANES:(k + 1) * LANES]
            nxt = pltpu.roll(_swap_halves(a), n_chunks - 1, 0)
            hidden.append(jax.nn.gelu(a + nxt + bias).astype(BF16))
        outs.append(_dot(jnp.concatenate(hidden, axis=1), w2_ref[s]))
    out_ref[...] = jnp.concatenate(outs, axis=1)[:c_out]


def _compress(pool, table, cmp_w1, cmp_b1, cmp_pe, cmp_w2, pg):
    nb, npg = table.shape
    per_page = PAGE_SIZE // CMP_STRIDE
    assert pg % 2 == 0, "pages are transposed in pairs"
    r = jnp.arange(2 * PAGE_SIZE)
    chunk, offset = r % (2 * per_page), r // (2 * per_page)
    source = (chunk // per_page) * PAGE_SIZE + (chunk % per_page) * CMP_STRIDE + offset
    perm = (r[None, :] == source[:, None]).astype(BF16)
    nj = npg // pg
    hid = cmp_w1.shape[-1]
    w = cmp_w1.reshape(2, CMP_LEN // CMP_STRIDE, CMP_STRIDE, HEAD_DIM, hid)
    base = w.transpose(0, 2, 3, 1, 4).reshape(2, CMP_STRIDE, HEAD_DIM, 2 * hid)
    eye = jnp.eye(N_KV, dtype=F32)
    wsl = jnp.einsum("ab,sldn->sladbn", eye, base).reshape(2, CMP_STRIDE, LANES, ROW_W)
    w1p = wsl.reshape(2, CMP_STRIDE * LANES, ROW_W).astype(BF16)
    w1e = jnp.pad(cmp_w1, ((0, 0), (0, 0), (0, LANES - hid))).astype(BF16)
    pe = jnp.broadcast_to(cmp_pe.reshape(2, 1, CMP_LEN * HEAD_DIM), (2, 8, CMP_LEN * HEAD_DIM)).astype(BF16)
    b1 = jnp.pad(cmp_b1, ((0, 0), (0, LANES - hid))).reshape(2, 1, LANES)
    w2h = jnp.pad(cmp_w2, ((0, 0), (0, LANES - hid), (0, 0)))
    w2c = jnp.einsum("ab,shd->sahbd", eye, w2h).reshape(2, ROW_W, N_KV * HEAD_DIM).astype(BF16)
    c_out = pg * PAGE_SIZE // CMP_STRIDE
    fix3 = lambda b, j, tab: (0, 0, 0)
    grid_spec = pltpu.PrefetchScalarGridSpec(
        num_scalar_prefetch=1,
        grid=(nb, nj),
        in_specs=[pl.BlockSpec(memory_space=pl.ANY), pl.BlockSpec(perm.shape, lambda b, j, tab: (0, 0)),
                  pl.BlockSpec(w1p.shape, fix3), pl.BlockSpec(w1e.shape, fix3),
                  pl.BlockSpec(pe.shape, fix3), pl.BlockSpec(b1.shape, fix3), pl.BlockSpec(w2c.shape, fix3)],
        out_specs=pl.BlockSpec((None, c_out, ROW_W), lambda b, j, tab: (b, j, 0)),
        scratch_shapes=[pltpu.VMEM((2, pg + 2, ROW_W, PAGE_SIZE), F32),
                        pltpu.VMEM((CMP_STRIDE, (pg + 2) * per_page, ROW_W), BF16),
                        pltpu.SemaphoreType.DMA((2,))],
    )
    return pl.pallas_call(
        functools.partial(_compress_kernel, pg=pg, nj=nj, npg=npg),
        grid_spec=grid_spec,
        out_shape=jax.ShapeDtypeStruct((nb, npg * per_page, ROW_W), F32),
        compiler_params=_params(("arbitrary", "arbitrary")),
    )(table, pool, perm, w1p, w1e, pe, b1, w2c)


def _overlap(ncp, nsp):
    ci = jnp.arange(ncp)[:, None] * CMP_STRIDE
    sj = jnp.arange(nsp)[None, :] * SLC_BLK
    return ((ci < sj + SLC_BLK) & (ci + CMP_LEN > sj)).astype(BF16)


def _cmp_prompt(q, kvc, batch, s_len, nc, tq):
    nq = q.shape[1]
    m = batch * s_len
    nqt = s_len // tq
    gw = nq // N_KV
    ncp = kvc.shape[1]
    ns = s_len // SLC_BLK
    assert ns <= LANES, "selection blocks of a prompt must fit one lane tile"
    return pl.pallas_call(
        functools.partial(_cmp_prompt_kernel, tq=tq, gp=gw // HEAD_DIM, nc=nc, ns=ns),
        grid=(batch, N_KV, nqt),
        in_specs=[pl.BlockSpec((tq, gw), lambda b, g, i: (b * nqt + i, g)),
                  pl.BlockSpec((None, ncp, ROW_W), lambda b, g, i: (b, 0, 0)),
                  pl.BlockSpec((LANES, ncp), lambda b, g, i: (0, 0))],
        out_specs=[pl.BlockSpec((tq, gw), lambda b, g, i: (b * nqt + i, g)),
                   pl.BlockSpec((None, tq, LANES), lambda b, g, i: (g, b * nqt + i, 0))],
        out_shape=[jax.ShapeDtypeStruct((m, nq), F32), jax.ShapeDtypeStruct((N_KV, m, LANES), BF16)],
        compiler_params=_params(("parallel", "parallel", "parallel")),
    )(q, kvc, _overlap(ncp, LANES).T)


TM = 512
TF = 1408
TQ = 256
TQ_SLC = 128
DEC_BS = 8
CMP_PG_SAMPLE = 64


def kernel(x_prompt, x_sample, cache_swa_a, cache_cmp, cache_slc, cache_win_b, page_table, a_norm1, a_wqkv, a_qnorm, a_knorm, a_sink, a_wo, a_norm2, a_w13, a_w2, kv_norm, kv_w, kv_knorm, cmp_pe, cmp_w1, cmp_b1, cmp_w2, b_norm1, b_wq, b_qnorm, b_wo, b_norm2, b_w13, b_w2):
    b_p, s_len, d = x_prompt.shape
    b_s, t_len, _ = x_sample.shape
    mp, ms = b_p * s_len, b_s * t_len
    n_pool = cache_cmp.shape[0]
    past_len = page_table.shape[1] * PAGE_SIZE
    assert t_len < CMP_STRIDE, "new tokens must not complete a compression chunk"
    pos = jnp.concatenate([jnp.tile(jnp.arange(s_len, dtype=jnp.int32), b_p),
                           jnp.tile(past_len + jnp.arange(t_len, dtype=jnp.int32), b_s)])
    tabs = _rot_tables(pos)
    h = jnp.concatenate([x_prompt.reshape(mp, d), x_sample.reshape(ms, d)], axis=0)
    row5 = lambda r, b: r.reshape(b, -1, 2, N_KV, HEAD_DIM)

    swa_p, swa_s = [], []
    for l in range(a_norm1.shape[0]):
        q, rows = _project_a(h, a_norm1[l], a_wqkv[l], a_qnorm[l], a_knorm[l], tabs, TM)
        rows_p, rows_s = rows[:mp], rows[mp:]
        o_p = _flash_prompt(q, rows, b_p, s_len, "band", WIN_A, TQ, 128, sink=a_sink[l])
        wa = cache_swa_a.shape[2]
        o_h = _window_decode(_to_qh(q[mp:], b_s), _tiles(cache_swa_a[l]), _new_tiles(rows_s, b_s), WIN_A, t_len,
                             DEC_BS, sink=a_sink[l])
        h = _mix([(o_p, _from_qh(o_h, t_len))], a_wo[l].astype(BF16), h, TM)
        swa_p.append(row5(rows_p, b_p)[:, -min(WIN_A, s_len):])
        swa_s.append(jnp.concatenate([cache_swa_a[l], row5(rows_s, b_s)], axis=1)[:, -wa:])
        h = _swiglu(h, a_norm2[l], a_w13[l].astype(BF16), a_w2[l].astype(BF16), TM, TF)

    cmp_rows, slc_rows, win_rows, cmp_t, slc_t, win_t = _project_kv(h, kv_norm, kv_w, kv_knorm, tabs, TM, b_p, s_len)
    cmp_s, slc_s, win_s = cmp_rows[mp:], slc_rows[mp:], win_rows[mp:]
    pages_p = s_len // PAGE_SIZE
    pool_p = cmp_t.reshape(b_p, ROW_W, pages_p, PAGE_SIZE).transpose(0, 2, 1, 3).reshape(b_p * pages_p, ROW_W, PAGE_SIZE)
    kvc_p = _compress(pool_p, jnp.arange(b_p * pages_p, dtype=jnp.int32).reshape(b_p, pages_p),
                      cmp_w1, cmp_b1, cmp_pe, cmp_w2, pages_p)
    kvc_s = _compress(_tiles(cache_cmp).reshape(n_pool, ROW_W, PAGE_SIZE), page_table, cmp_w1, cmp_b1, cmp_pe,
                      cmp_w2, CMP_PG_SAMPLE)
    nc_p = s_len // CMP_STRIDE - CMP_LEN // CMP_STRIDE + 1
    nc_s = past_len // CMP_STRIDE - CMP_LEN // CMP_STRIDE + 1
    ns_s = -(-(past_len + t_len) // SLC_BLK)
    wb = cache_win_b.shape[1]

    for j in range(b_norm1.shape[0]):
        q, gates = _project_bq(h, b_norm1[j], b_wq[j], b_qnorm[j], tabs, TM)
        oc_p, sel = _cmp_prompt(q, kvc_p, b_p, s_len, nc_p, TQ)
        os_p = _flash_prompt(q, slc_rows, b_p, s_len, "slc", 0, TQ_SLC, 256, sel=sel)
        ow_p = _flash_prompt(q, win_rows, b_p, s_len, "band", WIN_B, TQ, 256)
        qh = _to_qh(q[mp:], b_s)
        oc_z, idx = _cmp_decode(_to_qz(q[mp:], b_s), kvc_s, nc_s, ns_s, past_len, t_len, DEC_BS)
        os_h = _slc_decode(idx[..., :min(N_SEL, ns_s)], page_table, qh, _tiles(cache_slc), _new_tiles(slc_s, b_s),
                           past_len)
        ow_h = _window_decode(qh, _tiles(cache_win_b), _new_tiles(win_s, b_s), WIN_B, t_len, DEC_BS)
        h = _mix([(oc_p, _from_qz(oc_z, t_len)), (os_p, _from_qh(os_h, t_len)), (ow_p, _from_qh(ow_h, t_len))],
                 b_wo[j].astype(BF16), h, TM, gates=gates)
        h = _swiglu(h, b_norm2[j], b_w13[j].astype(BF16), b_w2[j].astype(BF16), TM, TF)

    hp = h[:mp].reshape(b_p, s_len, d)
    hs = h[mp:].reshape(b_s, t_len, d)
    from_tiles = lambda t: t.reshape(b_p, 2, N_KV, HEAD_DIM, -1).transpose(0, 4, 1, 2, 3)
    win_b_prompt = from_tiles(win_t[:, :, s_len - min(WIN_B, s_len):])
    win_b_sample = jnp.concatenate([cache_win_b, row5(win_s, b_s)], axis=1)[:, -wb:]
    return (hp, hs, jnp.stack(swa_p, axis=0), jnp.stack(swa_s, axis=0),
            from_tiles(cmp_t), row5(cmp_s, b_s), from_tiles(slc_t), row5(slc_s, b_s), win_b_prompt, win_b_sample)
```
